```python
import math
import jax, jax.numpy as jnp
from jax import lax
import numpy as np

D_MODEL = 1024
BATCH = 4
SEQ = 8192
DEPTH = 2

RMS_EPS = 1e-5
CONV_WIDTH = 4
MLSTM_HEADS = 4
MLSTM_DH = D_MODEL // (2 * MLSTM_HEADS)
MLSTM_W = MLSTM_HEADS * MLSTM_DH
MLSTM_CHUNK = 64
DIFF_HEADS = 4
DIFF_DH = D_MODEL // (4 * DIFF_HEADS)
DIFF_DV = 2 * DIFF_DH
DIFF_QK_W = DIFF_HEADS * DIFF_DH
DIFF_W = DIFF_HEADS * DIFF_DV
Q_BLOCK = 128
AB_SPLITS = (2 * MLSTM_W, MLSTM_W, MLSTM_W, MLSTM_HEADS, MLSTM_HEADS,
             2 * DIFF_QK_W, 2 * DIFF_QK_W, DIFF_W)
AB_IN = sum(AB_SPLITS)
AB_OUT_IN = MLSTM_W + DIFF_W
SSM_INNER = 2 * D_MODEL
SSM_HEADDIM = 64
SSM_HEADS = SSM_INNER // SSM_HEADDIM
SSM_STATE = 128
SSM_GROUPS = 8
SSM_HPG = SSM_HEADS // SSM_GROUPS
SSM_CONV_CH = SSM_INNER + 2 * SSM_GROUPS * SSM_STATE
SSM_IN = SSM_INNER + SSM_CONV_CH + SSM_HEADS
SSM_CHUNK = 64
D_FF = 4 * D_MODEL

kernel_name = "hybrid_mlstm_diffattn_mamba2"


def _split(a, sizes):
    idx = np.cumsum(np.array(sizes))[:-1].tolist()
    return jnp.split(a, idx, axis=-1)


def rmsnorm(x, g):
    xf = x.astype(jnp.float32)
    y = xf * lax.rsqrt(jnp.mean(xf * xf, axis=-1, keepdims=True) + RMS_EPS)
    return (y * g.astype(jnp.float32)).astype(x.dtype)


def causal_dwconv(x, w, b):
    y = lax.conv_general_dilated(x, w[:, None, :], window_strides=(1,),
                                 padding=[(CONV_WIDTH - 1, 0)],
                                 dimension_numbers=('NWC', 'WIO', 'NWC'),
                                 feature_group_count=x.shape[-1])
    return y + b


def alibi_slopes(n):
    return jnp.exp2(-8.0 * jnp.arange(1, n + 1, dtype=jnp.float32) / n)


def mlstm_chunkwise(q, k, v, i_pre, f_pre):
    bsz, nh, T, dh = q.shape
    L = MLSTM_CHUNK
    nc = T // L
    f32 = jnp.float32
    q = q.astype(f32).reshape(bsz, nh, nc, L, dh)
    k = (k.astype(f32) * dh ** -0.5).reshape(bsz, nh, nc, L, dh)
    v = v.astype(f32).reshape(bsz, nh, nc, L, dh)
    ig = i_pre.astype(f32).reshape(bsz, nh, nc, L)
    lf = jax.nn.log_sigmoid(f_pre.astype(f32)).reshape(bsz, nh, nc, L)
    b = jnp.cumsum(lf, axis=-1)
    b_last = b[..., -1]
    a = b_last[..., None] - b + ig
    m_loc = jnp.max(a, axis=-1)
    w = jnp.exp(a - m_loc[..., None])
    C_loc = jnp.einsum('bhcs,bhcsd,bhcse->bhcde', w, k, v)
    n_loc = jnp.einsum('bhcs,bhcsd->bhcd', w, k)

    def step(carry, inp):
        C, n, m = carry
        Cl, nl, ml, bl = inp
        m_new = jnp.maximum(bl + m, ml)
        sp = jnp.exp(bl + m - m_new)
        sl = jnp.exp(ml - m_new)
        C_new = sp[..., None, None] * C + sl[..., None, None] * Cl
        n_new = sp[..., None] * n + sl[..., None] * nl
        return (C_new, n_new, m_new), (C, n, m)

    init = (jnp.zeros((bsz, nh, dh, dh), f32), jnp.zeros((bsz, nh, dh), f32),
            jnp.zeros((bsz, nh), f32))
    xs = (jnp.moveaxis(C_loc, 2, 0), jnp.moveaxis(n_loc, 2, 0),
          jnp.moveaxis(m_loc, 2, 0), jnp.moveaxis(b_last, 2, 0))
    _, (C0, n0, m0) = lax.scan(step, init, xs)
    C0 = jnp.moveaxis(C0, 0, 2)
    n0 = jnp.moveaxis(n0, 0, 2)
    m0 = jnp.moveaxis(m0, 0, 2)
    causal = jnp.tril(jnp.ones((L, L), dtype=bool))
    logw = jnp.where(causal, b[..., :, None] - b[..., None, :] + ig[..., None, :], -jnp.inf)
    m_inter = b + m0[..., None]
    m_t = jnp.maximum(m_inter, jnp.max(logw, axis=-1))
    wts = jnp.exp(logw - m_t[..., None]) * jnp.einsum('bhctd,bhcsd->bhcts', q, k)
    s_inter = jnp.exp(m_inter - m_t)
    num = (jnp.einsum('bhcts,bhcse->bhcte', wts, v)
           + s_inter[..., None] * jnp.einsum('bhctd,bhcde->bhcte', q, C0))
    den = jnp.sum(wts, axis=-1) + s_inter * jnp.einsum('bhctd,bhcd->bhct', q, n0)
    h = num / jnp.maximum(jnp.abs(den), jnp.exp(-m_t))[..., None]
    return h.reshape(bsz, nh, T, dh)


def diff_attention(q, k, v, lam, slopes):
    _, bsz, nh, T, d = q.shape
    f32 = jnp.float32
    nb = T // Q_BLOCK
    q_blocks = jnp.moveaxis((q.astype(f32) * d ** -0.5).reshape(2, bsz, nh, nb, Q_BLOCK, d), 3, 0)
    kf = k.astype(f32)
    vf = v.astype(f32)
    starts = jnp.arange(nb, dtype=jnp.int32) * Q_BLOCK
    kpos = jnp.arange(T, dtype=jnp.int32)

    def one_block(args):
        qb, start = args
        dist = (start + jnp.arange(Q_BLOCK, dtype=jnp.int32))[:, None] - kpos[None, :]
        s = (jnp.einsum('mbhqd,mbhkd->mbhqk', qb, kf)
             - slopes[:, None, None] * dist.astype(f32))
        s = jnp.where(dist >= 0, s, -jnp.inf)
        p = jax.nn.softmax(s, axis=-1)
        return jnp.einsum('bhqk,bhke->bhqe', p[0] - lam * p[1], vf)

    out = lax.map(one_block, (q_blocks, starts))
    return jnp.moveaxis(out, 0, 2).reshape(bsz, nh, T, v.shape[-1])


def mlstm_diff_mixer(h, w_in, conv_w, conv_b, i_bias, f_bias, mnorm_g,
                     lq1, lk1, lq2, lk2, dnorm_g, w_out, lam_init):
    bsz, T, _ = h.shape
    mqk, mv, mo, mi, mf, dq, dk, dv = _split(h @ w_in, AB_SPLITS)
    mqk = jax.nn.silu(causal_dwconv(mqk, conv_w, conv_b))
    mq, mk = jnp.split(mqk, 2, axis=-1)
    heads = lambda t: t.reshape(bsz, T, MLSTM_HEADS, MLSTM_DH).transpose(0, 2, 1, 3)
    hm = mlstm_chunkwise(heads(mq), heads(mk), heads(mv),
                         (mi + i_bias).transpose(0, 2, 1), (mf + f_bias).transpose(0, 2, 1))
    hm = rmsnorm(hm.transpose(0, 2, 1, 3), mnorm_g.reshape(MLSTM_HEADS, MLSTM_DH))
    hm = (jax.nn.sigmoid(mo.astype(jnp.float32)) * hm.reshape(bsz, T, MLSTM_W)).astype(h.dtype)
    qk_heads = lambda t: t.reshape(bsz, T, DIFF_HEADS, 2, DIFF_DH).transpose(3, 0, 2, 1, 4)
    f32 = jnp.float32
    lam = (jnp.exp(jnp.sum(lq1.astype(f32) * lk1.astype(f32)))
           - jnp.exp(jnp.sum(lq2.astype(f32) * lk2.astype(f32))) + lam_init)
    hd = diff_attention(qk_heads(dq), qk_heads(dk),
                        dv.reshape(bsz, T, DIFF_HEADS, DIFF_DV).transpose(0, 2, 1, 3),
                        lam, alibi_slopes(DIFF_HEADS))
    hd = rmsnorm(hd.transpose(0, 2, 1, 3), dnorm_g.reshape(DIFF_HEADS, DIFF_DV)) * (1.0 - lam_init)
    hd = hd.reshape(bsz, T, DIFF_W).astype(h.dtype)
    return jnp.concatenate([hm, hd], axis=-1) @ w_out


def ssd_chunked(x, dt, A, Bm, Cm):
    bsz, T, G, R, P = x.shape
    N = Bm.shape[-1]
    L = SSM_CHUNK
    nc = T // L
    f32 = jnp.float32
    xdt = (x.astype(f32) * dt[..., None]).reshape(bsz, nc, L, G, R, P)
    a_cs = jnp.cumsum((dt * A).reshape(bsz, nc, L, G, R), axis=2)
    Bc = Bm.astype(f32).reshape(bsz, nc, L, G, N)
    Cc = Cm.astype(f32).reshape(bsz, nc, L, G, N)
    causal = jnp.tril(jnp.ones((L, L), dtype=bool))[:, :, None, None]
    seg = a_cs[:, :, :, None] - a_cs[:, :, None, :]
    decay = jnp.exp(jnp.where(causal, seg, -jnp.inf))
    cb = jnp.einsum('bctgn,bcsgn->bctsg', Cc, Bc)
    y_diag = jnp.einsum('bctsg,bctsgr,bcsgrp->bctgrp', cb, decay, xdt)

    def step(S, inp):
        Bk, Ck, xk, ak = inp
        y_off = jnp.einsum('blgn,bgrpn,blgr->blgrp', Ck, S, jnp.exp(ak))
        S_loc = jnp.einsum('blgn,blgr,blgrp->bgrpn', Bk, jnp.exp(ak[:, -1:] - ak), xk)
        S = jnp.exp(ak[:, -1])[..., None, None] * S + S_loc
        return S, y_off

    S0 = jnp.zeros((bsz, G, R, P, N), f32)
    xs = (jnp.moveaxis(Bc, 1, 0), jnp.moveaxis(Cc, 1, 0),
          jnp.moveaxis(xdt, 1, 0), jnp.moveaxis(a_cs, 1, 0))
    _, y_off = lax.scan(step, S0, xs)
    y = y_diag + jnp.moveaxis(y_off, 0, 1)
    return y.reshape(bsz, T, G, R, P)


def mamba2_mixer(h, w_in, conv_w, conv_b, dt_bias, a_log, d_skip, norm_g, w_out):
    bsz, T, _ = h.shape
    f32 = jnp.float32
    z, xbc, dt = _split(h @ w_in, (SSM_INNER, SSM_CONV_CH, SSM_HEADS))
    xbc = jax.nn.silu(causal_dwconv(xbc, conv_w, conv_b))
    xs, Bm, Cm = _split(xbc, (SSM_INNER, SSM_GROUPS * SSM_STATE, SSM_GROUPS * SSM_STATE))
    xs = xs.reshape(bsz, T, SSM_GROUPS, SSM_HPG, SSM_HEADDIM)
    dt = jax.nn.softplus(dt.astype(f32) + dt_bias.astype(f32)).reshape(bsz, T, SSM_GROUPS, SSM_HPG)
    A = -jnp.exp(a_log.astype(f32)).reshape(SSM_GROUPS, SSM_HPG)
    y = ssd_chunked(xs, dt, A, Bm.reshape(bsz, T, SSM_GROUPS, SSM_STATE),
                    Cm.reshape(bsz, T, SSM_GROUPS, SSM_STATE))
    y = y + d_skip.astype(f32).reshape(SSM_GROUPS, SSM_HPG, 1) * xs.astype(f32)
    y = y.reshape(bsz, T, SSM_INNER) * jax.nn.silu(z.astype(f32))
    y = rmsnorm(y.reshape(bsz, T, SSM_GROUPS, SSM_INNER // SSM_GROUPS),
                norm_g.reshape(SSM_GROUPS, SSM_INNER // SSM_GROUPS))
    return y.reshape(bsz, T, SSM_INNER).astype(h.dtype) @ w_out


def squared_relu_mlp(h, w1, w2):
    return jnp.square(jax.nn.relu(h @ w1)) @ w2


def setup_inputs(seed: int = 0) -> dict:
    key = jax.random.key(seed)
    ks = iter(jax.random.split(key, 40))
    f32 = jnp.float32
    nrm = lambda shape, scale: scale * jax.random.normal(next(ks), shape, f32)
    ne = (DEPTH + 1) // 2
    no = DEPTH // 2
    dt0 = jnp.exp(jax.random.uniform(next(ks), (no, SSM_HEADS), f32,
                                     minval=math.log(1e-3), maxval=math.log(1e-1)))
    return {
        'x': nrm((BATCH, SEQ, D_MODEL), 1.0),
        'norm_mix_g': 1.0 + nrm((DEPTH, D_MODEL), 0.02),
        'norm_mlp_g': 1.0 + nrm((DEPTH, D_MODEL), 0.02),
        'ab_w_in': nrm((ne, D_MODEL, AB_IN), D_MODEL ** -0.5),
        'ab_conv_w': nrm((ne, CONV_WIDTH, 2 * MLSTM_W), CONV_WIDTH ** -0.5),
        'ab_conv_b': nrm((ne, 2 * MLSTM_W), 0.01),
        'ab_i_bias': nrm((ne, MLSTM_HEADS), 0.1),
        'ab_f_bias': jnp.linspace(3.0, 6.0, MLSTM_HEADS, dtype=f32)[None] + nrm((ne, MLSTM_HEADS), 0.1),
        'ab_mnorm_g': 1.0 + nrm((ne, MLSTM_W), 0.02),
        'ab_lam_q1': nrm((ne, DIFF_DH), 0.1),
        'ab_lam_k1': nrm((ne, DIFF_DH), 0.1),
        'ab_lam_q2': nrm((ne, DIFF_DH), 0.1),
        'ab_lam_k2': nrm((ne, DIFF_DH), 0.1),
        'ab_dnorm_g': 1.0 + nrm((ne, DIFF_W), 0.02),
        'ab_w_out': nrm((ne, AB_OUT_IN, D_MODEL), AB_OUT_IN ** -0.5),
        'ssm_w_in': nrm((no, D_MODEL, SSM_IN), D_MODEL ** -0.5),
        'ssm_conv_w': nrm((no, CONV_WIDTH, SSM_CONV_CH), CONV_WIDTH ** -0.5),
        'ssm_conv_b': nrm((no, SSM_CONV_CH), 0.01),
        'ssm_dt_bias': dt0 + jnp.log(-jnp.expm1(-dt0)),
        'ssm_a_log': jnp.log(jax.random.uniform(next(ks), (no, SSM_HEADS), f32, minval=1.0, maxval=16.0)),
        'ssm_d': 1.0 + nrm((no, SSM_HEADS), 0.1),
        'ssm_norm_g': 1.0 + nrm((no, SSM_INNER), 0.02),
        'ssm_w_out': nrm((no, SSM_INNER, D_MODEL), SSM_INNER ** -0.5),
        'mlp_w1': nrm((DEPTH, D_MODEL, D_FF), D_MODEL ** -0.5),
        'mlp_w2': nrm((DEPTH, D_FF, D_MODEL), D_FF ** -0.5),
        'final_g': 1.0 + nrm((D_MODEL,), 0.02),
    }


def reference(x, norm_mix_g, norm_mlp_g, ab_w_in, ab_conv_w, ab_conv_b, ab_i_bias,
              ab_f_bias, ab_mnorm_g, ab_lam_q1, ab_lam_k1, ab_lam_q2, ab_lam_k2,
              ab_dnorm_g, ab_w_out, ssm_w_in, ssm_conv_w, ssm_conv_b, ssm_dt_bias,
              ssm_a_log, ssm_d, ssm_norm_g, ssm_w_out, mlp_w1, mlp_w2, final_g):
    for i in range(DEPTH):
        j = i // 2
        h = rmsnorm(x, norm_mix_g[i])
        if i % 2 == 0:
            lam_init = 0.8 - 0.6 * math.exp(-0.3 * i)
            x = x + mlstm_diff_mixer(h, ab_w_in[j], ab_conv_w[j], ab_conv_b[j], ab_i_bias[j],
                                     ab_f_bias[j], ab_mnorm_g[j], ab_lam_q1[j], ab_lam_k1[j],
                                     ab_lam_q2[j], ab_lam_k2[j], ab_dnorm_g[j], ab_w_out[j],
                                     lam_init)
        else:
            x = x + mamba2_mixer(h, ssm_w_in[j], ssm_conv_w[j], ssm_conv_b[j], ssm_dt_bias[j],
                                 ssm_a_log[j], ssm_d[j], ssm_norm_g[j], ssm_w_out[j])
        x = x + squared_relu_mlp(rmsnorm(x, norm_mlp_g[i]), mlp_w1[i], mlp_w2[i])
    return rmsnorm(x, final_g)
```

```python
import functools
import math

import jax
import jax.numpy as jnp
from jax import lax
from jax.experimental import pallas as pl
from jax.experimental.pallas import tpu as pltpu

F32 = jnp.float32
BF16 = jnp.bfloat16
HIGHEST = lax.Precision.HIGHEST

RMS_EPS = 1e-5
CONV_WIDTH = 4
LANES = 128
HALO = 8
NEG_BIG = -1e30
VMEM_LIMIT = 56 * 1024 * 1024

MLSTM_HEADS = 4
MLSTM_DH = 128
DIFF_HEADS = 4
DIFF_DH = 64
DIFF_DV = 128
SSM_GROUPS = 8
SSM_HPG = 4
SSM_HEADDIM = 64
SSM_STATE = 128
SSM_GW = SSM_HPG * SSM_HEADDIM


def _params(*sem):
    return pltpu.CompilerParams(dimension_semantics=sem, vmem_limit_bytes=VMEM_LIMIT)


def _sigmoid(x):
    return 1.0 / (1.0 + jnp.exp(-x))


def _rms(x, g):
    return x * lax.rsqrt(jnp.mean(x * x, axis=-1, keepdims=True) + RMS_EPS) * g


def _dot(a, b):
    return jnp.dot(a, b, preferred_element_type=F32)


def _dot_nt(a, b):
    return lax.dot_general(a, b, (((1,), (1,)), ((), ())), preferred_element_type=F32)


def _conv_silu(x, prev, w, b):
    rows = lax.broadcasted_iota(jnp.int32, (HALO, x.shape[1]), 0)
    y = x * w[CONV_WIDTH - 1:CONV_WIDTH] + b
    for s in range(1, CONV_WIDTH):
        xs = pltpu.roll(x, s, 0)
        head = jnp.where(rows < s, pltpu.roll(prev, s, 0), xs[:HALO])
        xs = jnp.concatenate([head, xs[HALO:]], axis=0)
        y = y + xs * w[CONV_WIDTH - 1 - s:CONV_WIDTH - s]
    return y * _sigmoid(y)


def _norm_matmul_kernel(x_ref, g_ref, w_ref, wg_ref, o_ref, og_ref, h_ref):
    @pl.when(pl.program_id(1) == 0)
    def _():
        hb = _rms(x_ref[...], g_ref[...]).astype(BF16)
        h_ref[...] = hb
        og_ref[...] = _dot(hb, wg_ref[...])
    o_ref[...] = _dot(h_ref[...], w_ref[...]).astype(o_ref.dtype)


def _norm_matmul(x, g, w, wg, *, tm, tn):
    M, D = x.shape
    N = w.shape[1]
    return pl.pallas_call(
        _norm_matmul_kernel,
        grid=(M // tm, N // tn),
        in_specs=[
            pl.BlockSpec((tm, D), lambda i, j: (i, 0)),
            pl.BlockSpec((1, D), lambda i, j: (0, 0)),
            pl.BlockSpec((D, tn), lambda i, j: (0, j)),
            pl.BlockSpec((D, LANES), lambda i, j: (0, 0)),
        ],
        out_specs=[
            pl.BlockSpec((tm, tn), lambda i, j: (i, j)),
            pl.BlockSpec((tm, LANES), lambda i, j: (i, 0)),
        ],
        out_shape=[jax.ShapeDtypeStruct((M, N), BF16), jax.ShapeDtypeStruct((M, LANES), F32)],
        scratch_shapes=[pltpu.VMEM((tm, D), BF16)],
        compiler_params=_params("parallel", "arbitrary"),
        name="norm_inproj",
    )(x, g, w, wg)


def _mlstm_kernel(q_ref, k_ref, v_ref, o_ref, gt_ref, cw_ref, cb_ref, gb_ref, ng_ref, out_ref,
                  c_st, n_st, m_st, qprev, kprev, *, L):
    H, DH = MLSTM_HEADS, MLSTM_DH
    W = H * DH

    @pl.when(pl.program_id(1) == 0)
    def _():
        c_st[...] = jnp.zeros_like(c_st)
        n_st[...] = jnp.zeros_like(n_st)
        m_st[...] = jnp.zeros_like(m_st)
        qprev[...] = jnp.zeros_like(qprev)
        kprev[...] = jnp.zeros_like(kprev)

    xq = q_ref[0].astype(F32)
    xk = k_ref[0].astype(F32)
    cw = cw_ref[...]
    cb = cb_ref[...]
    q = _conv_silu(xq, qprev[...], cw[:, :W], cb[:, :W])
    k = _conv_silu(xk, kprev[...], cw[:, W:], cb[:, W:]) * (DH ** -0.5)
    qprev[...] = xq[L - HALO:]
    kprev[...] = xk[L - HALO:]
    v = v_ref[0]
    og = o_ref[0].astype(F32)
    ng = ng_ref[...]

    gts = gt_ref[0] + gb_ref[...]
    lane = lax.broadcasted_iota(jnp.int32, (L, LANES), 1)
    logsig = jnp.minimum(gts, 0.0) - jnp.log1p(jnp.exp(-jnp.abs(gts)))
    G = jnp.where(lane < H, gts, logsig)
    GT = G.T
    row = lax.broadcasted_iota(jnp.int32, (L, L), 0)
    col = lax.broadcasted_iota(jnp.int32, (L, L), 1)
    causal = col <= row
    tri = causal.astype(F32)
    triT = (row <= col).astype(F32)
    Bc = jnp.dot(tri, G, precision=HIGHEST, preferred_element_type=F32)
    Br = jnp.dot(GT[0:2 * H], triT, precision=HIGHEST, preferred_element_type=F32)

    for h in range(H):
        sl = slice(h * DH, (h + 1) * DH)
        qh, kh, vh = q[:, sl], k[:, sl], v[:, sl]
        bc = Bc[:, H + h:H + h + 1]
        br = Br[H + h:H + h + 1, :]
        igr = GT[h:h + 1, :]
        igc = G[:, h:h + 1]
        m0 = m_st[h:h + 1, 0:1]
        c0 = c_st[h]
        n0 = n_st[h:h + 1, :]

        logw = jnp.where(causal, bc - br + igr, -jnp.inf)
        m_inter = bc + m0
        m_t = jnp.maximum(m_inter, jnp.max(logw, axis=-1, keepdims=True))
        qb = qh.astype(BF16)
        wts = jnp.exp(logw - m_t) * _dot_nt(qb, kh.astype(BF16))
        s_inter = jnp.exp(m_inter - m_t)
        num = _dot(wts.astype(BF16), vh) + s_inter * _dot(qb, c0.astype(BF16))
        den = (jnp.sum(wts, axis=-1, keepdims=True)
               + s_inter * jnp.sum(qh * n0, axis=-1, keepdims=True))
        hh = num / jnp.maximum(jnp.abs(den), jnp.exp(-m_t))

        b_last = bc[L - 1:L, :]
        m_loc = jnp.max(b_last - br + igr, axis=-1, keepdims=True)
        m_new = jnp.maximum(b_last + m0, m_loc)
        kw = kh * jnp.exp(b_last - bc + igc - m_new)
        sp = jnp.exp(b_last + m0 - m_new)
        c_st[h] = sp * c0 + _dot(kw.T.astype(BF16), vh)
        n_st[h:h + 1, :] = sp * n0 + jnp.sum(kw, axis=0, keepdims=True)
        m_st[h:h + 1, :] = jnp.broadcast_to(m_new, (1, LANES))

        out_ref[0, :, sl] = (_sigmoid(og[:, sl]) * _rms(hh, ng[:, sl])).astype(out_ref.dtype)


def _mlstm(proj, gates, conv_w, conv_b, gate_bias, norm_g, *, L):
    B, T, _ = proj.shape
    W = MLSTM_HEADS * MLSTM_DH
    blk = lambda c0: pl.BlockSpec((1, L, W), lambda b, c: (b, c, c0))
    full = lambda a: pl.BlockSpec(a.shape, lambda b, c: (0,) * a.ndim)
    return pl.pallas_call(
        functools.partial(_mlstm_kernel, L=L),
        grid=(B, T // L),
        in_specs=[blk(0), blk(1), blk(2), blk(3),
                  pl.BlockSpec((1, L, LANES), lambda b, c: (b, c, 0)),
                  full(conv_w), full(conv_b), full(gate_bias), full(norm_g)],
        out_specs=pl.BlockSpec((1, L, W), lambda b, c: (b, c, 0)),
        out_shape=jax.ShapeDtypeStruct((B, T, W), BF16),
        scratch_shapes=[pltpu.VMEM((MLSTM_HEADS, MLSTM_DH, MLSTM_DH), F32),
                        pltpu.VMEM((8, MLSTM_DH), F32),
                        pltpu.VMEM((8, LANES), F32),
                        pltpu.VMEM((HALO, W), F32),
                        pltpu.VMEM((HALO, W), F32)],
        compiler_params=_params("parallel", "arbitrary"),
        name="mlstm",
    )(proj, proj, proj, proj, gates, conv_w, conv_b, gate_bias, norm_g)


def _diff_attn_kernel(slopes_ref, lam_ref, q_ref, k_ref, v_ref, ng_ref, out_ref,
                      acc1, acc2, m1, l1, m2, l2, *, tq, lam_init):
    h = pl.program_id(1)
    qi = pl.program_id(2)
    slope = slopes_ref[h]

    qs = q_ref[0] * (DIFF_DH ** -0.5)
    lane = lax.broadcasted_iota(jnp.int32, qs.shape, 1)
    zero = jnp.zeros_like(qs)
    qz = (jnp.where(lane < DIFF_DH, qs, zero), jnp.where(lane >= DIFF_DH, qs, zero))
    maps = ((acc1, m1, l1), (acc2, m2, l2))
    for acc, m, l in maps:
        acc[...] = jnp.zeros_like(acc)
        m[...] = jnp.full_like(m, NEG_BIG)
        l[...] = jnp.zeros_like(l)

    kcol = lax.broadcasted_iota(jnp.int32, (1, tq), 1)
    row = lax.broadcasted_iota(jnp.int32, (tq, tq), 0)
    col = lax.broadcasted_iota(jnp.int32, (tq, tq), 1)

    def step(j, masked):
        start = pl.multiple_of(j * tq, tq)
        kb = k_ref[0, pl.ds(start, tq), :]
        vb = v_ref[0, pl.ds(start, tq), :]
        bias = slope * (kcol + (j - qi) * tq).astype(F32)
        for z, (acc, m, l) in zip(qz, maps):
            s = _dot_nt(z, kb) + bias
            if masked:
                s = jnp.where(col <= row, s, NEG_BIG)
            m_old = m[...]
            m_new = jnp.maximum(m_old, jnp.max(s, axis=-1, keepdims=True))
            alpha = jnp.exp(m_old - m_new)
            p = jnp.exp(s - m_new)
            l[...] = alpha * l[...] + jnp.sum(p, axis=-1, keepdims=True)
            acc[...] = alpha * acc[...] + _dot(p.astype(BF16), vb)
            m[...] = m_new

    def body(j, carry):
        step(j, False)
        return carry

    lax.fori_loop(0, qi, body, 0)
    step(qi, True)

    lp = lam_ref[...]
    lam = (jnp.exp(jnp.sum(lp[0:1] * lp[1:2], axis=-1, keepdims=True))
           - jnp.exp(jnp.sum(lp[2:3] * lp[3:4], axis=-1, keepdims=True)) + lam_init)
    o = acc1[...] / l1[...] - lam * (acc2[...] / l2[...])
    out_ref[0] = (_rms(o, ng_ref[...]) * (1.0 - lam_init)).astype(out_ref.dtype)


def _diff_attn(proj, slopes, lam_params, norm_g, *, tq, lam_init, q0, k0, v0):
    B, T, _ = proj.shape
    H = DIFF_HEADS
    return pl.pallas_call(
        functools.partial(_diff_attn_kernel, tq=tq, lam_init=lam_init),
        grid=(B, H, T // tq),
        in_specs=[
            pl.BlockSpec(memory_space=pltpu.SMEM),
            pl.BlockSpec(lam_params.shape, lambda b, h, i: (0, 0)),
            pl.BlockSpec((1, tq, LANES), lambda b, h, i: (b, i, q0 + h)),
            pl.BlockSpec((1, T, LANES), lambda b, h, i: (b, 0, k0 + h)),
            pl.BlockSpec((1, T, LANES), lambda b, h, i: (b, 0, v0 + h)),
            pl.BlockSpec((1, LANES), lambda b, h, i: (0, h)),
        ],
        out_specs=pl.BlockSpec((1, tq, LANES), lambda b, h, i: (b, i, h)),
        out_shape=jax.ShapeDtypeStruct((B, T, H * DIFF_DV), BF16),
        scratch_shapes=[pltpu.VMEM((tq, DIFF_DV), F32), pltpu.VMEM((tq, DIFF_DV), F32),
                        pltpu.VMEM((tq, 1), F32), pltpu.VMEM((tq, 1), F32),
                        pltpu.VMEM((tq, 1), F32), pltpu.VMEM((tq, 1), F32)],
        compiler_params=_params("parallel", "parallel", "arbitrary"),
        name="diff_attn",
    )(slopes, lam_params, proj, proj, proj, norm_g)


def _ssd_kernel(z_ref, x_ref, b_ref, c_ref, dt_ref, cwx_ref, cwb_ref, cwc_ref, cbx_ref, cbb_ref,
                cbc_ref, hp_ref, ng_ref, out_ref, s_st, xprev, bprev, cprev, art, *, L):
    R, P = SSM_HPG, SSM_HEADDIM
    g = pl.program_id(1)

    @pl.when(pl.program_id(2) == 0)
    def _():
        s_st[...] = jnp.zeros_like(s_st)
        xprev[...] = jnp.zeros_like(xprev)
        bprev[...] = jnp.zeros_like(bprev)
        cprev[...] = jnp.zeros_like(cprev)

    xr = x_ref[0].astype(F32)
    br = b_ref[0].astype(F32)
    cr = c_ref[0].astype(F32)
    x = _conv_silu(xr, xprev[...], cwx_ref[...], cbx_ref[...])
    bm = _conv_silu(br, bprev[...], cwb_ref[...], cbb_ref[...])
    cm = _conv_silu(cr, cprev[...], cwc_ref[...], cbc_ref[...])
    xprev[...] = xr[L - HALO:]
    bprev[...] = br[L - HALO:]
    cprev[...] = cr[L - HALO:]

    hp = hp_ref[...]
    pre = dt_ref[0] + hp[0:1]
    dt = jnp.maximum(pre, 0.0) + jnp.log1p(jnp.exp(-jnp.abs(pre)))
    da = dt * (-jnp.exp(hp[1:2]))
    row = lax.broadcasted_iota(jnp.int32, (L, L), 0)
    col = lax.broadcasted_iota(jnp.int32, (L, L), 1)
    causal = col <= row
    acs = jnp.dot(causal.astype(F32), da, precision=HIGHEST, preferred_element_type=F32)
    art[...] = acs.T
    lane = lax.broadcasted_iota(jnp.int32, (L, LANES), 1)
    lane1 = lax.broadcasted_iota(jnp.int32, (1, LANES), 1)

    cmb = cm.astype(BF16)
    bmb = bm.astype(BF16)
    cb = _dot_nt(cmb, bmb)
    s0 = s_st[...]
    y_off = _dot(cmb, s0.astype(BF16))

    ys, xws, decs = [], [], []
    for r in range(R):
        hd = g * R + r
        pick = lane == hd
        ac = jnp.sum(jnp.where(pick, acs, 0.0), axis=-1, keepdims=True)
        dtc = jnp.sum(jnp.where(pick, dt, 0.0), axis=-1, keepdims=True)
        dsk = jnp.sum(jnp.where(lane1 == hd, hp[2:3], 0.0), axis=-1, keepdims=True)
        ar = art[pl.ds(hd, 1), :]
        a_last = ac[L - 1:L, :]
        xh = x[:, r * P:(r + 1) * P]
        xdt = xh * dtc
        decay = jnp.exp(jnp.where(causal, ac - ar, -jnp.inf))
        y = _dot((cb * decay).astype(BF16), xdt.astype(BF16))
        y = y + jnp.exp(ac) * y_off[:, r * P:(r + 1) * P] + dsk * xh
        ys.append(y)
        xws.append(xdt * jnp.exp(a_last - ac))
        decs.append(jnp.broadcast_to(jnp.exp(a_last), (1, P)))

    xw = jnp.concatenate(xws, axis=1).astype(BF16)
    s_st[...] = jnp.concatenate(decs, axis=1) * s0 + _dot(bm.T.astype(BF16), xw)

    y = jnp.concatenate(ys, axis=1)
    zf = z_ref[0].astype(F32)
    y = y * (zf * _sigmoid(zf))
    out_ref[0] = _rms(y, ng_ref[...]).astype(out_ref.dtype)


def _ssd(proj, dtraw, conv_w, conv_b, head_params, norm_g, *, L):
    B, T, _ = proj.shape
    G, GW, N = SSM_GROUPS, SSM_GW, SSM_STATE
    inner = G * GW
    xo, bo, co = inner // GW, 2 * inner // N, (2 * inner + G * N) // N
    cwb0, cwc0 = inner // N, (inner + G * N) // N
    return pl.pallas_call(
        functools.partial(_ssd_kernel, L=L),
        grid=(B, G, T // L),
        in_specs=[
            pl.BlockSpec((1, L, GW), lambda b, g, c: (b, c, g)),
            pl.BlockSpec((1, L, GW), lambda b, g, c: (b, c, xo + g)),
            pl.BlockSpec((1, L, N), lambda b, g, c: (b, c, bo + g)),
            pl.BlockSpec((1, L, N), lambda b, g, c: (b, c, co + g)),
            pl.BlockSpec((1, L, LANES), lambda b, g, c: (b, c, 0)),
            pl.BlockSpec((CONV_WIDTH, GW), lambda b, g, c: (0, g)),
            pl.BlockSpec((CONV_WIDTH, N), lambda b, g, c: (0, cwb0 + g)),
            pl.BlockSpec((CONV_WIDTH, N), lambda b, g, c: (0, cwc0 + g)),
            pl.BlockSpec((1, GW), lambda b, g, c: (0, g)),
            pl.BlockSpec((1, N), lambda b, g, c: (0, cwb0 + g)),
            pl.BlockSpec((1, N), lambda b, g, c: (0, cwc0 + g)),
            pl.BlockSpec(head_params.shape, lambda b, g, c: (0, 0)),
            pl.BlockSpec((1, GW), lambda b, g, c: (0, g)),
        ],
        out_specs=pl.BlockSpec((1, L, GW), lambda b, g, c: (b, c, g)),
        out_shape=jax.ShapeDtypeStruct((B, T, inner), BF16),
        scratch_shapes=[pltpu.VMEM((N, GW), F32),
                        pltpu.VMEM((HALO, GW), F32),
                        pltpu.VMEM((HALO, N), F32),
                        pltpu.VMEM((HALO, N), F32),
                        pltpu.VMEM((LANES, L), F32)],
        compiler_params=_params("parallel", "parallel", "arbitrary"),
        name="ssd",
    )(proj, proj, proj, proj, dtraw, conv_w, conv_w, conv_w, conv_b, conv_b, conv_b,
      head_params, norm_g)


def _proj_residual_kernel(*refs, n_in):
    x_ref = refs[0]
    a_refs = refs[1:1 + n_in]
    w_refs = refs[1 + n_in:1 + 2 * n_in]
    o_ref = refs[1 + 2 * n_in]
    acc = x_ref[...]
    for a_ref, w_ref in zip(a_refs, w_refs):
        acc = acc + _dot(a_ref[...], w_ref[...])
    o_ref[...] = acc


def _proj_residual(x, acts, weights, *, tm):
    M, D = x.shape
    n_in = len(acts)
    return pl.pallas_call(
        functools.partial(_proj_residual_kernel, n_in=n_in),
        grid=(M // tm,),
        in_specs=([pl.BlockSpec((tm, D), lambda i: (i, 0))]
                  + [pl.BlockSpec((tm, a.shape[1]), lambda i: (i, 0)) for a in acts]
                  + [pl.BlockSpec(w.shape, lambda i: (0, 0)) for w in weights]),
        out_specs=pl.BlockSpec((tm, D), lambda i: (i, 0)),
        out_shape=jax.ShapeDtypeStruct((M, D), F32),
        compiler_params=_params("parallel"),
        name="outproj_residual",
    )(x, *acts, *weights)


def _mlp_kernel(x_ref, g_ref, w1_ref, w2_ref, fg_ref, o_ref, *, tf, final_norm):
    x = x_ref[...]
    hb = _rms(x, g_ref[...]).astype(BF16)
    acc = x
    for c in range(w1_ref.shape[1] // tf):
        u = jnp.maximum(_dot(hb, w1_ref[:, c * tf:(c + 1) * tf]), 0.0)
        acc = acc + _dot((u * u).astype(BF16), w2_ref[c * tf:(c + 1) * tf, :])
    if final_norm:
        acc = _rms(acc, fg_ref[...])
    o_ref[...] = acc


def _mlp(x, g, w1, w2, final_g, *, tm, tf, final_norm):
    M, D = x.shape
    const = lambda a: pl.BlockSpec(a.shape, lambda i: (0, 0), pipeline_mode=pl.Buffered(1))
    return pl.pallas_call(
        functools.partial(_mlp_kernel, tf=tf, final_norm=final_norm),
        grid=(M // tm,),
        in_specs=[pl.BlockSpec((tm, D), lambda i: (i, 0)), const(g), const(w1), const(w2),
                  const(final_g)],
        out_specs=pl.BlockSpec((tm, D), lambda i: (i, 0)),
        out_shape=jax.ShapeDtypeStruct((M, D), F32),
        compiler_params=_params("parallel"),
        name="mlp",
    )(x, g, w1, w2, final_g)


def _pad_cols(a, n):
    return jnp.pad(a, ((0, 0), (0, n - a.shape[1])))


def _tile(n, pref):
    return min(n, pref)


def kernel(x, norm_mix_g, norm_mlp_g, ab_w_in, ab_conv_w, ab_conv_b, ab_i_bias, ab_f_bias, ab_mnorm_g, ab_lam_q1, ab_lam_k1, ab_lam_q2, ab_lam_k2, ab_dnorm_g, ab_w_out, ssm_w_in, ssm_conv_w, ssm_conv_b, ssm_dt_bias, ssm_a_log, ssm_d, ssm_norm_g, ssm_w_out, mlp_w1, mlp_w2, final_g):
    B, T, D = x.shape
    M = B * T
    depth = norm_mix_g.shape[0]
    tm_proj = _tile(M, 1024)
    tm_mlp = _tile(M, 512)
    row = lambda a: a.reshape(1, -1).astype(F32)
    xf = x.reshape(M, D)

    for i in range(depth):
        j = i // 2
        if i % 2 == 0:
            w = ab_w_in[j]
            mw = 2 * MLSTM_HEADS * MLSTM_DH
            g0 = 2 * mw
            w_main = jnp.concatenate([w[:, :g0], w[:, g0 + 2 * MLSTM_HEADS:]], axis=1).astype(BF16)
            w_gate = _pad_cols(w[:, g0:g0 + 2 * MLSTM_HEADS], LANES).astype(BF16)
            proj, gates = _norm_matmul(xf, row(norm_mix_g[i]), w_main, w_gate, tm=tm_proj, tn=512)
            proj = proj.reshape(B, T, -1)
            gates = gates.reshape(B, T, LANES)
            gate_bias = _pad_cols(jnp.concatenate([ab_i_bias[j], ab_f_bias[j]]).reshape(1, -1), LANES)
            hm = _mlstm(proj, gates, ab_conv_w[j], row(ab_conv_b[j]), gate_bias,
                        row(ab_mnorm_g[j]), L=_tile(T, 128))
            lam_init = 0.8 - 0.6 * math.exp(-0.3 * i)
            slopes = jnp.exp2(-8.0 * jnp.arange(1, DIFF_HEADS + 1, dtype=F32) / DIFF_HEADS)
            lam_params = jnp.stack([ab_lam_q1[j], ab_lam_k1[j], ab_lam_q2[j], ab_lam_k2[j]])
            qblk = g0 // LANES
            hd = _diff_attn(proj, slopes, lam_params, row(ab_dnorm_g[j]), tq=_tile(T, 512),
                            lam_init=lam_init, q0=qblk, k0=qblk + DIFF_HEADS, v0=qblk + 2 * DIFF_HEADS)
            wo = ab_w_out[j].astype(BF16)
            half = MLSTM_HEADS * MLSTM_DH
            xf = _proj_residual(xf, [hm.reshape(M, -1), hd.reshape(M, -1)], [wo[:half], wo[half:]],
                                tm=tm_proj)
        else:
            w = ssm_w_in[j]
            n_main = w.shape[1] - SSM_GROUPS * SSM_HPG
            w_main = w[:, :n_main].astype(BF16)
            w_dt = _pad_cols(w[:, n_main:], LANES).astype(BF16)
            proj, dtraw = _norm_matmul(xf, row(norm_mix_g[i]), w_main, w_dt, tm=tm_proj, tn=512)
            head_params = _pad_cols(jnp.stack([ssm_dt_bias[j], ssm_a_log[j], ssm_d[j]]), LANES)
            y = _ssd(proj.reshape(B, T, -1), dtraw.reshape(B, T, LANES), ssm_conv_w[j],
                     row(ssm_conv_b[j]), head_params, row(ssm_norm_g[j]), L=_tile(T, 256))
            xf = _proj_residual(xf, [y.reshape(M, -1)], [ssm_w_out[j].astype(BF16)], tm=tm_proj)
        xf = _mlp(xf, row(norm_mlp_g[i]), mlp_w1[i].astype(BF16), mlp_w2[i].astype(BF16),
                  row(final_g), tm=tm_mlp, tf=512, final_norm=(i == depth - 1))
    return xf.reshape(B, T, D)
```

```python
import functools
import math

import jax
import jax.numpy as jnp
from jax import lax
from jax.experimental import pallas as pl
from jax.experimental.pallas import tpu as pltpu

F32 = jnp.float32
BF16 = jnp.bfloat16
HIGHEST = lax.Precision.HIGHEST

RMS_EPS = 1e-5
CONV_WIDTH = 4
LANES = 128
HALO = 8
NEG_BIG = -1e30
ACC_PAD = 16
VMEM_LIMIT = 56 * 1024 * 1024

MLSTM_HEADS = 4
MLSTM_DH = 128
DIFF_HEADS = 4
DIFF_DH = 64
DIFF_DV = 128
SSM_GROUPS = 8
SSM_HPG = 4
SSM_HEADDIM = 64
SSM_STATE = 128
SSM_GW = SSM_HPG * SSM_HEADDIM


def _params(*sem):
    return pltpu.CompilerParams(dimension_semantics=sem, vmem_limit_bytes=VMEM_LIMIT)


def _sigmoid(x):
    return 1.0 / (1.0 + jnp.exp(-x))


def _rms(x, g):
    return x * lax.rsqrt(jnp.mean(x * x, axis=-1, keepdims=True) + RMS_EPS) * g


def _dot(a, b):
    return jnp.dot(a, b, preferred_element_type=F32)


def _dot_nt(a, b):
    return lax.dot_general(a, b, (((1,), (1,)), ((), ())), preferred_element_type=F32)


def _conv_silu(x, prev, w, b):
    rows = lax.broadcasted_iota(jnp.int32, (HALO, x.shape[1]), 0)
    y = x * w[CONV_WIDTH - 1:CONV_WIDTH] + b
    for s in range(1, CONV_WIDTH):
        xs = pltpu.roll(x, s, 0)
        head = jnp.where(rows < s, pltpu.roll(prev, s, 0), xs[:HALO])
        xs = jnp.concatenate([head, xs[HALO:]], axis=0)
        y = y + xs * w[CONV_WIDTH - 1 - s:CONV_WIDTH - s]
    return y * _sigmoid(y)


def _norm_matmul_kernel(x_ref, g_ref, w_ref, wg_ref, o_ref, og_ref, h_ref):
    @pl.when(pl.program_id(1) == 0)
    def _():
        hb = _rms(x_ref[...], g_ref[...]).astype(BF16)
        h_ref[...] = hb
        og_ref[...] = _dot(hb, wg_ref[...])
    o_ref[...] = _dot(h_ref[...], w_ref[...]).astype(o_ref.dtype)


def _norm_matmul(x, g, w, wg, *, tm, tn):
    M, D = x.shape
    N = w.shape[1]
    return pl.pallas_call(
        _norm_matmul_kernel,
        grid=(M // tm, N // tn),
        in_specs=[
            pl.BlockSpec((tm, D), lambda i, j: (i, 0)),
            pl.BlockSpec((1, D), lambda i, j: (0, 0)),
            pl.BlockSpec((D, tn), lambda i, j: (0, j)),
            pl.BlockSpec((D, LANES), lambda i, j: (0, 0)),
        ],
        out_specs=[
            pl.BlockSpec((tm, tn), lambda i, j: (i, j)),
            pl.BlockSpec((tm, LANES), lambda i, j: (i, 0)),
        ],
        out_shape=[jax.ShapeDtypeStruct((M, N), BF16), jax.ShapeDtypeStruct((M, LANES), F32)],
        scratch_shapes=[pltpu.VMEM((tm, D), BF16)],
        compiler_params=_params("parallel", "arbitrary"),
        name="norm_inproj",
    )(x, g, w, wg)


def _mlstm_kernel(q_ref, k_ref, v_ref, o_ref, gt_ref, cw_ref, cb_ref, gb_ref, ng_ref, out_ref,
                  c_st, n_st, m_st, qprev, kprev, *, L):
    H, DH = MLSTM_HEADS, MLSTM_DH
    W = H * DH

    @pl.when(pl.program_id(1) == 0)
    def _():
        c_st[...] = jnp.zeros_like(c_st)
        n_st[...] = jnp.zeros_like(n_st)
        m_st[...] = jnp.zeros_like(m_st)
        qprev[...] = jnp.zeros_like(qprev)
        kprev[...] = jnp.zeros_like(kprev)

    xq = q_ref[0].astype(F32)
    xk = k_ref[0].astype(F32)
    cw = cw_ref[...]
    cb = cb_ref[...]
    q = _conv_silu(xq, qprev[...], cw[:, :W], cb[:, :W])
    k = _conv_silu(xk, kprev[...], cw[:, W:], cb[:, W:]) * (DH ** -0.5)
    qprev[...] = xq[L - HALO:]
    kprev[...] = xk[L - HALO:]
    v = v_ref[0]
    og = o_ref[0].astype(F32)
    ng = ng_ref[...]

    gts = gt_ref[0] + gb_ref[...]
    lane = lax.broadcasted_iota(jnp.int32, (L, LANES), 1)
    logsig = jnp.minimum(gts, 0.0) - jnp.log1p(jnp.exp(-jnp.abs(gts)))
    G = jnp.where(lane < H, gts, logsig)
    GT = G.T
    row = lax.broadcasted_iota(jnp.int32, (L, L), 0)
    col = lax.broadcasted_iota(jnp.int32, (L, L), 1)
    causal = col <= row
    tri = causal.astype(F32)
    triT = (row <= col).astype(F32)
    Bc = jnp.dot(tri, G, precision=HIGHEST, preferred_element_type=F32)
    Br = jnp.dot(GT[0:2 * H], triT, precision=HIGHEST, preferred_element_type=F32)

    for h in range(H):
        sl = slice(h * DH, (h + 1) * DH)
        qh, kh, vh = q[:, sl], k[:, sl], v[:, sl]
        bc = Bc[:, H + h:H + h + 1]
        br = Br[H + h:H + h + 1, :]
        igr = GT[h:h + 1, :]
        igc = G[:, h:h + 1]
        m0 = m_st[h:h + 1, 0:1]
        c0 = c_st[h]
        n0 = n_st[h:h + 1, :]

        logw = jnp.where(causal, bc - br + igr, -jnp.inf)
        m_inter = bc + m0
        m_t = jnp.maximum(m_inter, jnp.max(logw, axis=-1, keepdims=True))
        qb = qh.astype(BF16)
        wts = jnp.exp(logw - m_t) * _dot_nt(qb, kh.astype(BF16))
        s_inter = jnp.exp(m_inter - m_t)
        num = _dot(wts.astype(BF16), vh) + s_inter * _dot(qb, c0.astype(BF16))
        den = (jnp.sum(wts, axis=-1, keepdims=True)
               + s_inter * jnp.sum(qh * n0, axis=-1, keepdims=True))
        hh = num / jnp.maximum(jnp.abs(den), jnp.exp(-m_t))

        b_last = bc[L - 1:L, :]
        m_loc = jnp.max(b_last - br + igr, axis=-1, keepdims=True)
        m_new = jnp.maximum(b_last + m0, m_loc)
        kw = kh * jnp.exp(b_last - bc + igc - m_new)
        sp = jnp.exp(b_last + m0 - m_new)
        c_st[h] = sp * c0 + _dot(kw.T.astype(BF16), vh)
        n_st[h:h + 1, :] = sp * n0 + jnp.sum(kw, axis=0, keepdims=True)
        m_st[h:h + 1, :] = jnp.broadcast_to(m_new, (1, LANES))

        out_ref[0, :, sl] = (_sigmoid(og[:, sl]) * _rms(hh, ng[:, sl])).astype(out_ref.dtype)


def _mlstm(proj, gates, conv_w, conv_b, gate_bias, norm_g, *, L):
    B, T, _ = proj.shape
    W = MLSTM_HEADS * MLSTM_DH
    blk = lambda c0: pl.BlockSpec((1, L, W), lambda b, c: (b, c, c0))
    full = lambda a: pl.BlockSpec(a.shape, lambda b, c: (0,) * a.ndim)
    return pl.pallas_call(
        functools.partial(_mlstm_kernel, L=L),
        grid=(B, T // L),
        in_specs=[blk(0), blk(1), blk(2), blk(3),
                  pl.BlockSpec((1, L, LANES), lambda b, c: (b, c, 0)),
                  full(conv_w), full(conv_b), full(gate_bias), full(norm_g)],
        out_specs=pl.BlockSpec((1, L, W), lambda b, c: (b, c, 0)),
        out_shape=jax.ShapeDtypeStruct((B, T, W), BF16),
        scratch_shapes=[pltpu.VMEM((MLSTM_HEADS, MLSTM_DH, MLSTM_DH), F32),
                        pltpu.VMEM((8, MLSTM_DH), F32),
                        pltpu.VMEM((8, LANES), F32),
                        pltpu.VMEM((HALO, W), F32),
                        pltpu.VMEM((HALO, W), F32)],
        compiler_params=_params("parallel", "arbitrary"),
        name="mlstm",
    )(proj, proj, proj, proj, gates, conv_w, conv_b, gate_bias, norm_g)


def _diff_attn_kernel(slopes_ref, lam_ref, q_ref, k_ref, v_ref, ng_ref, out_ref,
                      vT, ta, tb, acc1, acc2, m1, m2, *, tq, lam_init):
    h = pl.program_id(1)
    qi = pl.program_id(2)
    slope = slopes_ref[h]
    half, dv = DIFF_DH, DIFF_DV

    vT[qi, 0:dv, :] = v_ref[0, pl.ds(pl.multiple_of(qi * tq, tq), tq), :].T
    ones_row = lax.broadcasted_iota(jnp.int32, (ACC_PAD, tq), 0) == 0
    vT[qi, dv:dv + ACC_PAD, :] = jnp.where(ones_row, 1.0, 0.0).astype(BF16)

    lane = lax.broadcasted_iota(jnp.int32, (tq, LANES), 1)
    r = lax.broadcasted_iota(jnp.int32, (tq, LANES), 0)
    r_lo = (r & 255).astype(F32)
    r_hi = (r - (r & 255)).astype(F32)
    sv = jnp.full((tq, LANES), slope, F32)
    s_hi = sv.astype(BF16).astype(F32)
    s_mid = (sv - s_hi).astype(BF16).astype(F32)
    s_lo = (sv - s_hi - s_mid).astype(BF16).astype(F32)
    qs = q_ref[0] * (half ** -0.5)

    def operands(f0):
        fl = lane - f0
        kfeat = jnp.where((fl >= 0) & (fl < 3), r_lo, jnp.where((fl >= 3) & (fl < 6), r_hi, 0.0))
        third = jnp.where(fl >= 3, fl - 3, fl)
        qfeat = jnp.where(third == 0, s_hi, jnp.where(third == 1, s_mid, s_lo))
        qfeat = jnp.where((fl >= 0) & (fl < 6), qfeat, 0.0)
        keep = (lane < half) if f0 >= half else (lane >= half)
        return kfeat.astype(BF16), keep, jnp.where(keep, qs, qfeat.astype(BF16))

    ops = (operands(half), operands(0))
    maps = ((acc1, m1), (acc2, m2))
    for acc, m in maps:
        acc[...] = jnp.zeros_like(acc)
        m[...] = jnp.full_like(m, NEG_BIG)

    krow = lax.broadcasted_iota(jnp.int32, (tq, tq), 0)
    qcol = lax.broadcasted_iota(jnp.int32, (tq, tq), 1)

    def scores(j, mi):
        kb = k_ref[0, pl.ds(pl.multiple_of(j * tq, tq), tq), :]
        kfeat, keep, qop = ops[mi]
        return _dot_nt(jnp.where(keep, kb, kfeat), qop)

    def step(j, cur, nxt, masked):
        vTb = vT[j]
        cst = slope * (jnp.zeros((1, tq), jnp.int32) + (j - qi) * tq).astype(F32)
        for mi, (acc, m) in enumerate(maps):
            if nxt is not None:
                nxt[mi] = scores(j + 1, mi)
            t = cur[mi]
            if masked:
                t = jnp.where(krow <= qcol, t, NEG_BIG)
            m_old = m[...]
            m_new = jnp.maximum(m_old, jnp.max(t, axis=0, keepdims=True) + cst)
            p = jnp.exp(t - (m_new - cst)).astype(BF16)
            acc[...] = jnp.exp(m_old - m_new) * acc[...] + _dot(vTb, p)
            m[...] = m_new

    for mi in range(2):
        ta[mi] = scores(0, mi)

    def pair(i, carry):
        step(2 * i, ta, tb, False)
        step(2 * i + 1, tb, ta, False)
        return carry

    lax.fori_loop(0, lax.shift_right_logical(qi, 1), pair, 0)
    odd = (qi & 1) == 1

    @pl.when(odd)
    def _():
        step(qi - 1, ta, tb, False)
        step(qi, tb, None, True)

    @pl.when(jnp.logical_not(odd))
    def _():
        step(qi, ta, None, True)

    lp = lam_ref[...]
    lam = (jnp.exp(jnp.sum(lp[0:1] * lp[1:2], axis=-1, keepdims=True))
           - jnp.exp(jnp.sum(lp[2:3] * lp[3:4], axis=-1, keepdims=True)) + lam_init)
    a1 = acc1[...]
    a2 = acc2[...]
    o = (a1[:dv] / a1[dv:dv + 1] - lam * (a2[:dv] / a2[dv:dv + 1])).T
    out_ref[0] = (_rms(o, ng_ref[...]) * (1.0 - lam_init)).astype(out_ref.dtype)


def _diff_attn(proj, slopes, lam_params, norm_g, *, tq, lam_init, q0, k0, v0):
    B, T, _ = proj.shape
    H = DIFF_HEADS
    return pl.pallas_call(
        functools.partial(_diff_attn_kernel, tq=tq, lam_init=lam_init),
        grid=(B, H, T // tq),
        in_specs=[
            pl.BlockSpec(memory_space=pltpu.SMEM),
            pl.BlockSpec(lam_params.shape, lambda b, h, i: (0, 0)),
            pl.BlockSpec((1, tq, LANES), lambda b, h, i: (b, i, q0 + h)),
            pl.BlockSpec((1, T, LANES), lambda b, h, i: (b, 0, k0 + h)),
            pl.BlockSpec((1, T, LANES), lambda b, h, i: (b, 0, v0 + h)),
            pl.BlockSpec((1, LANES), lambda b, h, i: (0, h)),
        ],
        out_specs=pl.BlockSpec((1, tq, LANES), lambda b, h, i: (b, i, h)),
        out_shape=jax.ShapeDtypeStruct((B, T, H * DIFF_DV), BF16),
        scratch_shapes=[pltpu.VMEM((T // tq, DIFF_DV + ACC_PAD, tq), BF16),
                        pltpu.VMEM((2, tq, tq), F32), pltpu.VMEM((2, tq, tq), F32),
                        pltpu.VMEM((DIFF_DV + ACC_PAD, tq), F32),
                        pltpu.VMEM((DIFF_DV + ACC_PAD, tq), F32),
                        pltpu.VMEM((1, tq), F32), pltpu.VMEM((1, tq), F32)],
        compiler_params=_params("parallel", "parallel", "arbitrary"),
        name="diff_attn",
    )(slopes, lam_params, proj, proj, proj, norm_g)


def _ssd_kernel(z_ref, x_ref, b_ref, c_ref, dt_ref, cwx_ref, cwb_ref, cwc_ref, cbx_ref, cbb_ref,
                cbc_ref, hp_ref, ng_ref, out_ref, s_st, xprev, bprev, cprev, art, *, L):
    R, P = SSM_HPG, SSM_HEADDIM
    g = pl.program_id(1)

    @pl.when(pl.program_id(2) == 0)
    def _():
        s_st[...] = jnp.zeros_like(s_st)
        xprev[...] = jnp.zeros_like(xprev)
        bprev[...] = jnp.zeros_like(bprev)
        cprev[...] = jnp.zeros_like(cprev)

    xr = x_ref[0].astype(F32)
    br = b_ref[0].astype(F32)
    cr = c_ref[0].astype(F32)
    x = _conv_silu(xr, xprev[...], cwx_ref[...], cbx_ref[...])
    bm = _conv_silu(br, bprev[...], cwb_ref[...], cbb_ref[...])
    cm = _conv_silu(cr, cprev[...], cwc_ref[...], cbc_ref[...])
    xprev[...] = xr[L - HALO:]
    bprev[...] = br[L - HALO:]
    cprev[...] = cr[L - HALO:]

    hp = hp_ref[...]
    pre = dt_ref[0] + hp[0:1]
    dt = jnp.maximum(pre, 0.0) + jnp.log1p(jnp.exp(-jnp.abs(pre)))
    da = dt * (-jnp.exp(hp[1:2]))
    row = lax.broadcasted_iota(jnp.int32, (L, L), 0)
    col = lax.broadcasted_iota(jnp.int32, (L, L), 1)
    causal = col <= row
    acs = jnp.dot(causal.astype(F32), da, precision=HIGHEST, preferred_element_type=F32)
    art[...] = acs.T
    lane = lax.broadcasted_iota(jnp.int32, (L, LANES), 1)
    lane1 = lax.broadcasted_iota(jnp.int32, (1, LANES), 1)

    cmb = cm.astype(BF16)
    bmb = bm.astype(BF16)
    cb = _dot_nt(cmb, bmb)
    s0 = s_st[...]
    y_off = _dot(cmb, s0.astype(BF16))

    ys, xws, decs = [], [], []
    for r in range(R):
        hd = g * R + r
        pick = lane == hd
        ac = jnp.sum(jnp.where(pick, acs, 0.0), axis=-1, keepdims=True)
        dtc = jnp.sum(jnp.where(pick, dt, 0.0), axis=-1, keepdims=True)
        dsk = jnp.sum(jnp.where(lane1 == hd, hp[2:3], 0.0), axis=-1, keepdims=True)
        ar = art[pl.ds(hd, 1), :]
        a_last = ac[L - 1:L, :]
        xh = x[:, r * P:(r + 1) * P]
        xdt = xh * dtc
        decay = jnp.exp(jnp.where(causal, ac - ar, -jnp.inf))
        y = _dot((cb * decay).astype(BF16), xdt.astype(BF16))
        y = y + jnp.exp(ac) * y_off[:, r * P:(r + 1) * P] + dsk * xh
        ys.append(y)
        xws.append(xdt * jnp.exp(a_last - ac))
        decs.append(jnp.broadcast_to(jnp.exp(a_last), (1, P)))

    xw = jnp.concatenate(xws, axis=1).astype(BF16)
    s_st[...] = jnp.concatenate(decs, axis=1) * s0 + _dot(bm.T.astype(BF16), xw)

    y = jnp.concatenate(ys, axis=1)
    zf = z_ref[0].astype(F32)
    y = y * (zf * _sigmoid(zf))
    out_ref[0] = _rms(y, ng_ref[...]).astype(out_ref.dtype)


def _ssd(proj, dtraw, conv_w, conv_b, head_params, norm_g, *, L):
    B, T, _ = proj.shape
    G, GW, N = SSM_GROUPS, SSM_GW, SSM_STATE
    inner = G * GW
    xo, bo, co = inner // GW, 2 * inner // N, (2 * inner + G * N) // N
    cwb0, cwc0 = inner // N, (inner + G * N) // N
    return pl.pallas_call(
        functools.partial(_ssd_kernel, L=L),
        grid=(B, G, T // L),
        in_specs=[
            pl.BlockSpec((1, L, GW), lambda b, g, c: (b, c, g)),
            pl.BlockSpec((1, L, GW), lambda b, g, c: (b, c, xo + g)),
            pl.BlockSpec((1, L, N), lambda b, g, c: (b, c, bo + g)),
            pl.BlockSpec((1, L, N), lambda b, g, c: (b, c, co + g)),
            pl.BlockSpec((1, L, LANES), lambda b, g, c: (b, c, 0)),
            pl.BlockSpec((CONV_WIDTH, GW), lambda b, g, c: (0, g)),
            pl.BlockSpec((CONV_WIDTH, N), lambda b, g, c: (0, cwb0 + g)),
            pl.BlockSpec((CONV_WIDTH, N), lambda b, g, c: (0, cwc0 + g)),
            pl.BlockSpec((1, GW), lambda b, g, c: (0, g)),
            pl.BlockSpec((1, N), lambda b, g, c: (0, cwb0 + g)),
            pl.BlockSpec((1, N), lambda b, g, c: (0, cwc0 + g)),
            pl.BlockSpec(head_params.shape, lambda b, g, c: (0, 0)),
            pl.BlockSpec((1, GW), lambda b, g, c: (0, g)),
        ],
        out_specs=pl.BlockSpec((1, L, GW), lambda b, g, c: (b, c, g)),
        out_shape=jax.ShapeDtypeStruct((B, T, inner), BF16),
        scratch_shapes=[pltpu.VMEM((N, GW), F32),
                        pltpu.VMEM((HALO, GW), F32),
                        pltpu.VMEM((HALO, N), F32),
                        pltpu.VMEM((HALO, N), F32),
                        pltpu.VMEM((LANES, L), F32)],
        compiler_params=_params("parallel", "parallel", "arbitrary"),
        name="ssd",
    )(proj, proj, proj, proj, dtraw, conv_w, conv_w, conv_w, conv_b, conv_b, conv_b,
      head_params, norm_g)


def _proj_residual_kernel(*refs, n_in):
    x_ref = refs[0]
    a_refs = refs[1:1 + n_in]
    w_refs = refs[1 + n_in:1 + 2 * n_in]
    o_ref = refs[1 + 2 * n_in]
    acc = x_ref[...]
    for a_ref, w_ref in zip(a_refs, w_refs):
        acc = acc + _dot(a_ref[...], w_ref[...])
    o_ref[...] = acc


def _proj_residual(x, acts, weights, *, tm):
    M, D = x.shape
    n_in = len(acts)
    return pl.pallas_call(
        functools.partial(_proj_residual_kernel, n_in=n_in),
        grid=(M // tm,),
        in_specs=([pl.BlockSpec((tm, D), lambda i: (i, 0))]
                  + [pl.BlockSpec((tm, a.shape[1]), lambda i: (i, 0)) for a in acts]
                  + [pl.BlockSpec(w.shape, lambda i: (0, 0)) for w in weights]),
        out_specs=pl.BlockSpec((tm, D), lambda i: (i, 0)),
        out_shape=jax.ShapeDtypeStruct((M, D), F32),
        compiler_params=_params("parallel"),
        name="outproj_residual",
    )(x, *acts, *weights)


def _mlp_kernel(x_ref, g_ref, w1_ref, w2_ref, fg_ref, o_ref, *, tf, final_norm):
    x = x_ref[...]
    hb = _rms(x, g_ref[...]).astype(BF16)
    acc = x
    for c in range(w1_ref.shape[1] // tf):
        u = jnp.maximum(_dot(hb, w1_ref[:, c * tf:(c + 1) * tf]), 0.0)
        acc = acc + _dot((u * u).astype(BF16), w2_ref[c * tf:(c + 1) * tf, :])
    if final_norm:
        acc = _rms(acc, fg_ref[...])
    o_ref[...] = acc


def _mlp(x, g, w1, w2, final_g, *, tm, tf, final_norm):
    M, D = x.shape
    const = lambda a: pl.BlockSpec(a.shape, lambda i: (0, 0), pipeline_mode=pl.Buffered(1))
    return pl.pallas_call(
        functools.partial(_mlp_kernel, tf=tf, final_norm=final_norm),
        grid=(M // tm,),
        in_specs=[pl.BlockSpec((tm, D), lambda i: (i, 0)), const(g), const(w1), const(w2),
                  const(final_g)],
        out_specs=pl.BlockSpec((tm, D), lambda i: (i, 0)),
        out_shape=jax.ShapeDtypeStruct((M, D), F32),
        compiler_params=_params("parallel"),
        name="mlp",
    )(x, g, w1, w2, final_g)


def _pad_cols(a, n):
    return jnp.pad(a, ((0, 0), (0, n - a.shape[1])))


def _tile(n, pref):
    return min(n, pref)


def kernel(x, norm_mix_g, norm_mlp_g, ab_w_in, ab_conv_w, ab_conv_b, ab_i_bias, ab_f_bias, ab_mnorm_g, ab_lam_q1, ab_lam_k1, ab_lam_q2, ab_lam_k2, ab_dnorm_g, ab_w_out, ssm_w_in, ssm_conv_w, ssm_conv_b, ssm_dt_bias, ssm_a_log, ssm_d, ssm_norm_g, ssm_w_out, mlp_w1, mlp_w2, final_g):
    B, T, D = x.shape
    M = B * T
    depth = norm_mix_g.shape[0]
    tm_proj = _tile(M, 1024)
    tm_mlp = _tile(M, 512)
    row = lambda a: a.reshape(1, -1).astype(F32)
    xf = x.reshape(M, D)

    for i in range(depth):
        j = i // 2
        if i % 2 == 0:
            w = ab_w_in[j]
            mw = 2 * MLSTM_HEADS * MLSTM_DH
            g0 = 2 * mw
            w_main = jnp.concatenate([w[:, :g0], w[:, g0 + 2 * MLSTM_HEADS:]], axis=1).astype(BF16)
            w_gate = _pad_cols(w[:, g0:g0 + 2 * MLSTM_HEADS], LANES).astype(BF16)
            proj, gates = _norm_matmul(xf, row(norm_mix_g[i]), w_main, w_gate, tm=tm_proj, tn=512)
            proj = proj.reshape(B, T, -1)
            gates = gates.reshape(B, T, LANES)
            gate_bias = _pad_cols(jnp.concatenate([ab_i_bias[j], ab_f_bias[j]]).reshape(1, -1), LANES)
            hm = _mlstm(proj, gates, ab_conv_w[j], row(ab_conv_b[j]), gate_bias,
                        row(ab_mnorm_g[j]), L=_tile(T, 128))
            lam_init = 0.8 - 0.6 * math.exp(-0.3 * i)
            slopes = jnp.exp2(-8.0 * jnp.arange(1, DIFF_HEADS + 1, dtype=F32) / DIFF_HEADS)
            lam_params = jnp.stack([ab_lam_q1[j], ab_lam_k1[j], ab_lam_q2[j], ab_lam_k2[j]])
            qblk = g0 // LANES
            hd = _diff_attn(proj, slopes, lam_params, row(ab_dnorm_g[j]), tq=_tile(T, 512),
                            lam_init=lam_init, q0=qblk, k0=qblk + DIFF_HEADS, v0=qblk + 2 * DIFF_HEADS)
            wo = ab_w_out[j].astype(BF16)
            half = MLSTM_HEADS * MLSTM_DH
            xf = _proj_residual(xf, [hm.reshape(M, -1), hd.reshape(M, -1)], [wo[:half], wo[half:]],
                                tm=tm_proj)
        else:
            w = ssm_w_in[j]
            n_main = w.shape[1] - SSM_GROUPS * SSM_HPG
            w_main = w[:, :n_main].astype(BF16)
            w_dt = _pad_cols(w[:, n_main:], LANES).astype(BF16)
            proj, dtraw = _norm_matmul(xf, row(norm_mix_g[i]), w_main, w_dt, tm=tm_proj, tn=512)
            head_params = _pad_cols(jnp.stack([ssm_dt_bias[j], ssm_a_log[j], ssm_d[j]]), LANES)
            y = _ssd(proj.reshape(B, T, -1), dtraw.reshape(B, T, LANES), ssm_conv_w[j],
                     row(ssm_conv_b[j]), head_params, row(ssm_norm_g[j]), L=_tile(T, 256))
            xf = _proj_residual(xf, [y.reshape(M, -1)], [ssm_w_out[j].astype(BF16)], tm=tm_proj)
        xf = _mlp(xf, row(norm_mlp_g[i]), mlp_w1[i].astype(BF16), mlp_w2[i].astype(BF16),
                  row(final_g), tm=tm_mlp, tf=512, final_norm=(i == depth - 1))
    return xf.reshape(B, T, D)
```

```python
import functools
import math

import jax
import jax.numpy as jnp
from jax import lax
from jax.experimental import pallas as pl
from jax.experimental.pallas import tpu as pltpu

F32 = jnp.float32
BF16 = jnp.bfloat16
HIGHEST = lax.Precision.HIGHEST

RMS_EPS = 1e-5
LOG2E = 1.4426950408889634
CONV_WIDTH = 4
LANES = 128
HALO = 8
NEG_BIG = -1e30
ACC_PAD = 16
VMEM_LIMIT = 56 * 1024 * 1024

MLSTM_HEADS = 4
MLSTM_DH = 128
DIFF_HEADS = 4
DIFF_DH = 64
DIFF_DV = 128
SSM_GROUPS = 8
SSM_HPG = 4
SSM_HEADDIM = 64
SSM_STATE = 128
SSM_GW = SSM_HPG * SSM_HEADDIM


def _params(*sem):
    return pltpu.CompilerParams(dimension_semantics=sem, vmem_limit_bytes=VMEM_LIMIT)


def _sigmoid(x):
    return 1.0 / (1.0 + jnp.exp2(x * -LOG2E))


def _rms(x, g):
    return x * lax.rsqrt(jnp.mean(x * x, axis=-1, keepdims=True) + RMS_EPS) * g


def _dot(a, b):
    return jnp.dot(a, b, preferred_element_type=F32)


def _dot_nt(a, b):
    return lax.dot_general(a, b, (((1,), (1,)), ((), ())), preferred_element_type=F32)


def _conv_silu_buf(buf, x, w, b, L):
    buf[HALO:HALO + L, :] = x
    y = x * w[CONV_WIDTH - 1:CONV_WIDTH] + b
    for s in range(1, CONV_WIDTH):
        y = y + buf[HALO - s:HALO - s + L, :] * w[CONV_WIDTH - 1 - s:CONV_WIDTH - s]
    buf[0:HALO, :] = x[L - HALO:]
    return y * _sigmoid(y)


def _norm_matmul_kernel(x_ref, g_ref, w_ref, wg_ref, o_ref, og_ref, *, tn):
    hb = _rms(x_ref[...], g_ref[...]).astype(BF16)
    og_ref[...] = _dot(hb, wg_ref[...])
    for c in range(w_ref.shape[1] // tn):
        o_ref[:, c * tn:(c + 1) * tn] = _dot(hb, w_ref[:, c * tn:(c + 1) * tn]).astype(o_ref.dtype)


def _norm_matmul(x, g, w, wg, *, tm, tn):
    M, D = x.shape
    N = w.shape[1]
    const = lambda a: pl.BlockSpec(a.shape, lambda i: (0, 0), pipeline_mode=pl.Buffered(1))
    return pl.pallas_call(
        functools.partial(_norm_matmul_kernel, tn=tn),
        grid=(M // tm,),
        in_specs=[pl.BlockSpec((tm, D), lambda i: (i, 0)), const(g), const(w), const(wg)],
        out_specs=[
            pl.BlockSpec((tm, N), lambda i: (i, 0)),
            pl.BlockSpec((tm, LANES), lambda i: (i, 0)),
        ],
        out_shape=[jax.ShapeDtypeStruct((M, N), BF16), jax.ShapeDtypeStruct((M, LANES), F32)],
        compiler_params=_params("parallel"),
        name="norm_inproj",
    )(x, g, w, wg)


def _mlstm_kernel(q_ref, k_ref, v_ref, o_ref, gt_ref, cw_ref, cb_ref, gb_ref, ng_ref, out_ref,
                  c_st, n_st, m_st, qbuf, kbuf, *, L):
    H, DH = MLSTM_HEADS, MLSTM_DH
    W = H * DH

    @pl.when(pl.program_id(1) == 0)
    def _():
        c_st[...] = jnp.zeros_like(c_st)
        n_st[...] = jnp.zeros_like(n_st)
        m_st[...] = jnp.zeros_like(m_st)
        qbuf[0:HALO, :] = jnp.zeros((HALO, W), F32)
        kbuf[0:HALO, :] = jnp.zeros((HALO, W), F32)

    cw = cw_ref[...]
    cb = cb_ref[...]
    q = _conv_silu_buf(qbuf, q_ref[0].astype(F32), cw[:, :W], cb[:, :W], L)
    k = _conv_silu_buf(kbuf, k_ref[0].astype(F32), cw[:, W:], cb[:, W:], L) * (DH ** -0.5)
    v = v_ref[0]
    og = o_ref[0].astype(F32)
    ng = ng_ref[...]

    gts = gt_ref[0] + gb_ref[...]
    lane = lax.broadcasted_iota(jnp.int32, (L, LANES), 1)
    logsig = jnp.minimum(gts, 0.0) - jnp.log1p(jnp.exp(-jnp.abs(gts)))
    G = jnp.where(lane < H, gts, logsig)
    GT = G.T
    row = lax.broadcasted_iota(jnp.int32, (L, L), 0)
    col = lax.broadcasted_iota(jnp.int32, (L, L), 1)
    causal = col <= row
    tri = causal.astype(F32)
    triT = (row <= col).astype(F32)
    Bc = jnp.dot(tri, G, precision=HIGHEST, preferred_element_type=F32)
    Br = jnp.dot(GT[0:2 * H], triT, precision=HIGHEST, preferred_element_type=F32)

    for h in range(H):
        sl = slice(h * DH, (h + 1) * DH)
        qh, kh, vh = q[:, sl], k[:, sl], v[:, sl]
        bc = Bc[:, H + h:H + h + 1]
        br = Br[H + h:H + h + 1, :]
        igr = GT[h:h + 1, :]
        igc = G[:, h:h + 1]
        m0 = m_st[h:h + 1, 0:1]
        c0 = c_st[h]
        n0 = n_st[h:h + 1, :]

        logw = jnp.where(causal, bc - br + igr, -jnp.inf)
        m_inter = bc + m0
        m_t = jnp.maximum(m_inter, jnp.max(logw, axis=-1, keepdims=True))
        qb = qh.astype(BF16)
        wts = jnp.exp(logw - m_t) * _dot_nt(qb, kh.astype(BF16))
        s_inter = jnp.exp(m_inter - m_t)
        num = _dot(wts.astype(BF16), vh) + s_inter * _dot(qb, c0.astype(BF16))
        den = (jnp.sum(wts, axis=-1, keepdims=True)
               + s_inter * jnp.sum(qh * n0, axis=-1, keepdims=True))
        hh = num / jnp.maximum(jnp.abs(den), jnp.exp(-m_t))

        b_last = bc[L - 1:L, :]
        m_loc = jnp.max(b_last - br + igr, axis=-1, keepdims=True)
        m_new = jnp.maximum(b_last + m0, m_loc)
        kw = kh * jnp.exp(b_last - bc + igc - m_new)
        sp = jnp.exp(b_last + m0 - m_new)
        c_st[h] = sp * c0 + _dot(kw.T.astype(BF16), vh)
        n_st[h:h + 1, :] = sp * n0 + jnp.sum(kw, axis=0, keepdims=True)
        m_st[h:h + 1, :] = jnp.broadcast_to(m_new, (1, LANES))

        out_ref[0, :, sl] = (_sigmoid(og[:, sl]) * _rms(hh, ng[:, sl])).astype(out_ref.dtype)


def _mlstm(proj, gates, conv_w, conv_b, gate_bias, norm_g, *, L):
    B, T, _ = proj.shape
    W = MLSTM_HEADS * MLSTM_DH
    blk = lambda c0: pl.BlockSpec((1, L, W), lambda b, c: (b, c, c0))
    full = lambda a: pl.BlockSpec(a.shape, lambda b, c: (0,) * a.ndim)
    return pl.pallas_call(
        functools.partial(_mlstm_kernel, L=L),
        grid=(B, T // L),
        in_specs=[blk(0), blk(1), blk(2), blk(3),
                  pl.BlockSpec((1, L, LANES), lambda b, c: (b, c, 0)),
                  full(conv_w), full(conv_b), full(gate_bias), full(norm_g)],
        out_specs=pl.BlockSpec((1, L, W), lambda b, c: (b, c, 0)),
        out_shape=jax.ShapeDtypeStruct((B, T, W), BF16),
        scratch_shapes=[pltpu.VMEM((MLSTM_HEADS, MLSTM_DH, MLSTM_DH), F32),
                        pltpu.VMEM((8, MLSTM_DH), F32),
                        pltpu.VMEM((8, LANES), F32),
                        pltpu.VMEM((HALO + L, W), F32),
                        pltpu.VMEM((HALO + L, W), F32)],
        compiler_params=_params("parallel", "arbitrary"),
        name="mlstm",
    )(proj, proj, proj, proj, gates, conv_w, conv_b, gate_bias, norm_g)


def _diff_attn_kernel(slopes_ref, lam_ref, q_ref, k_ref, v_ref, ng_ref, out_ref,
                      vT, ta, tb, acc1, acc2, m1, m2, *, tq, lam_init):
    h = pl.program_id(1)
    qi = pl.program_id(2)
    slope = slopes_ref[h]
    half, dv = DIFF_DH, DIFF_DV

    vT[qi, 0:dv, :] = v_ref[0, pl.ds(pl.multiple_of(qi * tq, tq), tq), :].T
    ones_row = lax.broadcasted_iota(jnp.int32, (ACC_PAD, tq), 0) == 0
    vT[qi, dv:dv + ACC_PAD, :] = jnp.where(ones_row, 1.0, 0.0).astype(BF16)

    lane = lax.broadcasted_iota(jnp.int32, (tq, LANES), 1)
    r = lax.broadcasted_iota(jnp.int32, (tq, LANES), 0)
    r_lo = (r & 255).astype(F32)
    r_hi = (r - (r & 255)).astype(F32)
    sv = jnp.full((tq, LANES), slope, F32)
    s_hi = sv.astype(BF16).astype(F32)
    s_mid = (sv - s_hi).astype(BF16).astype(F32)
    s_lo = (sv - s_hi - s_mid).astype(BF16).astype(F32)
    qs = q_ref[0] * (half ** -0.5)

    def operands(f0):
        fl = lane - f0
        kfeat = jnp.where((fl >= 0) & (fl < 3), r_lo, jnp.where((fl >= 3) & (fl < 6), r_hi, 0.0))
        third = jnp.where(fl >= 3, fl - 3, fl)
        qfeat = jnp.where(third == 0, s_hi, jnp.where(third == 1, s_mid, s_lo))
        qfeat = jnp.where((fl >= 0) & (fl < 6), qfeat, 0.0)
        keep = (lane < half) if f0 >= half else (lane >= half)
        return kfeat.astype(BF16), keep, jnp.where(keep, qs, qfeat.astype(BF16))

    ops = (operands(half), operands(0))
    maps = ((acc1, m1), (acc2, m2))
    for acc, m in maps:
        acc[...] = jnp.zeros_like(acc)
        m[...] = jnp.full_like(m, NEG_BIG)

    krow = lax.broadcasted_iota(jnp.int32, (tq, tq), 0)
    qcol = lax.broadcasted_iota(jnp.int32, (tq, tq), 1)

    def scores(j, mi):
        kb = k_ref[0, pl.ds(pl.multiple_of(j * tq, tq), tq), :]
        kfeat, keep, qop = ops[mi]
        return _dot_nt(jnp.where(keep, kb, kfeat), qop)

    def step(j, cur, nxt, masked):
        vTb = vT[j]
        cst = slope * (jnp.zeros((1, tq), jnp.int32) + (j - qi) * tq).astype(F32)
        for mi, (acc, m) in enumerate(maps):
            if nxt is not None:
                nxt[mi] = scores(j + 1, mi)
            t = cur[mi]
            if masked:
                t = jnp.where(krow <= qcol, t, NEG_BIG)
            m_old = m[...]
            m_new = jnp.maximum(m_old, jnp.max(t, axis=0, keepdims=True) + cst)
            p = jnp.exp(t - (m_new - cst)).astype(BF16)
            acc[...] = jnp.exp(m_old - m_new) * acc[...] + _dot(vTb, p)
            m[...] = m_new

    for mi in range(2):
        ta[mi] = scores(0, mi)

    def pair(i, carry):
        step(2 * i, ta, tb, False)
        step(2 * i + 1, tb, ta, False)
        return carry

    lax.fori_loop(0, lax.shift_right_logical(qi, 1), pair, 0)
    odd = (qi & 1) == 1

    @pl.when(odd)
    def _():
        step(qi - 1, ta, tb, False)
        step(qi, tb, None, True)

    @pl.when(jnp.logical_not(odd))
    def _():
        step(qi, ta, None, True)

    lp = lam_ref[...]
    lam = (jnp.exp(jnp.sum(lp[0:1] * lp[1:2], axis=-1, keepdims=True))
           - jnp.exp(jnp.sum(lp[2:3] * lp[3:4], axis=-1, keepdims=True)) + lam_init)
    a1 = acc1[...]
    a2 = acc2[...]
    o = (a1[:dv] / a1[dv:dv + 1] - lam * (a2[:dv] / a2[dv:dv + 1])).T
    out_ref[0] = (_rms(o, ng_ref[...]) * (1.0 - lam_init)).astype(out_ref.dtype)


def _diff_attn(proj, slopes, lam_params, norm_g, *, tq, lam_init, q0, k0, v0):
    B, T, _ = proj.shape
    H = DIFF_HEADS
    return pl.pallas_call(
        functools.partial(_diff_attn_kernel, tq=tq, lam_init=lam_init),
        grid=(B, H, T // tq),
        in_specs=[
            pl.BlockSpec(memory_space=pltpu.SMEM),
            pl.BlockSpec(lam_params.shape, lambda b, h, i: (0, 0)),
            pl.BlockSpec((1, tq, LANES), lambda b, h, i: (b, i, q0 + h)),
            pl.BlockSpec((1, T, LANES), lambda b, h, i: (b, 0, k0 + h)),
            pl.BlockSpec((1, T, LANES), lambda b, h, i: (b, 0, v0 + h)),
            pl.BlockSpec((1, LANES), lambda b, h, i: (0, h)),
        ],
        out_specs=pl.BlockSpec((1, tq, LANES), lambda b, h, i: (b, i, h)),
        out_shape=jax.ShapeDtypeStruct((B, T, H * DIFF_DV), BF16),
        scratch_shapes=[pltpu.VMEM((T // tq, DIFF_DV + ACC_PAD, tq), BF16),
                        pltpu.VMEM((2, tq, tq), F32), pltpu.VMEM((2, tq, tq), F32),
                        pltpu.VMEM((DIFF_DV + ACC_PAD, tq), F32),
                        pltpu.VMEM((DIFF_DV + ACC_PAD, tq), F32),
                        pltpu.VMEM((1, tq), F32), pltpu.VMEM((1, tq), F32)],
        compiler_params=_params("parallel", "parallel", "arbitrary"),
        name="diff_attn",
    )(slopes, lam_params, proj, proj, proj, norm_g)


def _ssd_kernel(z_ref, x_ref, b_ref, c_ref, dt_ref, cwx_ref, cwb_ref, cwc_ref, cbx_ref, cbb_ref,
                cbc_ref, hp_ref, ng_ref, out_ref, s_st, xbuf, bbuf, cbuf, dt_s, acs_s, art_s, *, L):
    R, P, GW = SSM_HPG, SSM_HEADDIM, SSM_GW
    g = pl.program_id(2)

    @pl.when(pl.program_id(1) == 0)
    def _():
        s_st[g] = jnp.zeros(s_st.shape[1:], F32)
        xbuf[g, 0:HALO, :] = jnp.zeros((HALO, xbuf.shape[2]), F32)
        bbuf[g, 0:HALO, :] = jnp.zeros((HALO, bbuf.shape[2]), F32)
        cbuf[g, 0:HALO, :] = jnp.zeros((HALO, cbuf.shape[2]), F32)

    row = lax.broadcasted_iota(jnp.int32, (L, L), 0)
    col = lax.broadcasted_iota(jnp.int32, (L, L), 1)
    causal = col <= row

    @pl.when(g == 0)
    def _():
        hp = hp_ref[...]
        pre = dt_ref[0] + hp[0:1]
        dt = jnp.maximum(pre, 0.0) + jnp.log1p(jnp.exp(-jnp.abs(pre)))
        da = dt * (-jnp.exp(hp[1:2]))
        acs = jnp.dot(causal.astype(F32), da, precision=HIGHEST, preferred_element_type=F32)
        dt_s[...] = dt
        acs_s[...] = acs
        art_s[...] = acs.T

    x = _conv_silu_buf(xbuf.at[g], x_ref[0].astype(F32), cwx_ref[...], cbx_ref[...], L)
    bm = _conv_silu_buf(bbuf.at[g], b_ref[0].astype(F32), cwb_ref[...], cbb_ref[...], L)
    cm = _conv_silu_buf(cbuf.at[g], c_ref[0].astype(F32), cwc_ref[...], cbc_ref[...], L)

    shift = (LANES - R * g) & (LANES - 1)
    dtg = pltpu.roll(dt_s[...], shift, 1)
    acg = pltpu.roll(acs_s[...], shift, 1)
    dsk = pltpu.roll(hp_ref[...], shift, 1)[2:3]
    er = lax.broadcasted_iota(jnp.int32, (LANES, GW), 0)
    ec = lax.broadcasted_iota(jnp.int32, (LANES, GW), 1)
    expand = ((ec >= er * P) & (ec < (er + 1) * P)).astype(F32)
    spread = jnp.dot(jnp.concatenate([dtg, acg, jnp.broadcast_to(dsk, (HALO, LANES))], axis=0),
                     expand, precision=HIGHEST, preferred_element_type=F32)
    dt_f, ac_f, d_f = spread[:L], spread[L:2 * L], spread[2 * L:2 * L + 1]
    a_last = ac_f[L - 1:L, :]

    cmb = cm.astype(BF16)
    cb = _dot_nt(cmb, bm.astype(BF16))
    s0 = s_st[g]
    xdt = x * dt_f
    xdt_b = xdt.astype(BF16)
    y = jnp.exp(ac_f) * _dot(cmb, s0.astype(BF16)) + d_f * x
    lane = lax.broadcasted_iota(jnp.int32, (L, GW), 1)
    for r in range(R):
        ac = acg[:, r:r + 1]
        ar = art_s[pl.ds(g * R + r, 1), :]
        decay = jnp.exp(jnp.where(causal, ac - ar, -jnp.inf))
        own = (lane >= r * P) & (lane < (r + 1) * P)
        y = y + _dot((cb * decay).astype(BF16), jnp.where(own, xdt_b, jnp.zeros_like(xdt_b)))

    xw = (xdt * jnp.exp(a_last - ac_f)).astype(BF16)
    s_st[g] = jnp.exp(a_last) * s0 + _dot(bm.T.astype(BF16), xw)

    zf = z_ref[0].astype(F32)
    y = y * (zf * _sigmoid(zf))
    out_ref[0] = _rms(y, ng_ref[...]).astype(out_ref.dtype)


def _ssd(proj, dtraw, conv_w, conv_b, head_params, norm_g, *, L):
    B, T, _ = proj.shape
    G, GW, N = SSM_GROUPS, SSM_GW, SSM_STATE
    inner = G * GW
    xo, bo, co = inner // GW, 2 * inner // N, (2 * inner + G * N) // N
    cwb0, cwc0 = inner // N, (inner + G * N) // N
    return pl.pallas_call(
        functools.partial(_ssd_kernel, L=L),
        grid=(B, T // L, G),
        in_specs=[
            pl.BlockSpec((1, L, GW), lambda b, c, g: (b, c, g)),
            pl.BlockSpec((1, L, GW), lambda b, c, g: (b, c, xo + g)),
            pl.BlockSpec((1, L, N), lambda b, c, g: (b, c, bo + g)),
            pl.BlockSpec((1, L, N), lambda b, c, g: (b, c, co + g)),
            pl.BlockSpec((1, L, LANES), lambda b, c, g: (b, c, 0)),
            pl.BlockSpec((CONV_WIDTH, GW), lambda b, c, g: (0, g)),
            pl.BlockSpec((CONV_WIDTH, N), lambda b, c, g: (0, cwb0 + g)),
            pl.BlockSpec((CONV_WIDTH, N), lambda b, c, g: (0, cwc0 + g)),
            pl.BlockSpec((1, GW), lambda b, c, g: (0, g)),
            pl.BlockSpec((1, N), lambda b, c, g: (0, cwb0 + g)),
            pl.BlockSpec((1, N), lambda b, c, g: (0, cwc0 + g)),
            pl.BlockSpec(head_params.shape, lambda b, c, g: (0, 0)),
            pl.BlockSpec((1, GW), lambda b, c, g: (0, g)),
        ],
        out_specs=pl.BlockSpec((1, L, GW), lambda b, c, g: (b, c, g)),
        out_shape=jax.ShapeDtypeStruct((B, T, inner), BF16),
        scratch_shapes=[pltpu.VMEM((G, N, GW), F32),
                        pltpu.VMEM((G, HALO + L, GW), F32),
                        pltpu.VMEM((G, HALO + L, N), F32),
                        pltpu.VMEM((G, HALO + L, N), F32),
                        pltpu.VMEM((L, LANES), F32),
                        pltpu.VMEM((L, LANES), F32),
                        pltpu.VMEM((LANES, L), F32)],
        compiler_params=_params("parallel", "arbitrary", "arbitrary"),
        name="ssd",
    )(proj, proj, proj, proj, dtraw, conv_w, conv_w, conv_w, conv_b, conv_b, conv_b,
      head_params, norm_g)


def _proj_residual_kernel(*refs, n_in):
    x_ref = refs[0]
    a_refs = refs[1:1 + n_in]
    w_refs = refs[1 + n_in:1 + 2 * n_in]
    o_ref = refs[1 + 2 * n_in]
    acc = x_ref[...]
    for a_ref, w_ref in zip(a_refs, w_refs):
        acc = acc + _dot(a_ref[...], w_ref[...])
    o_ref[...] = acc


def _proj_residual(x, acts, weights, *, tm):
    M, D = x.shape
    n_in = len(acts)
    return pl.pallas_call(
        functools.partial(_proj_residual_kernel, n_in=n_in),
        grid=(M // tm,),
        in_specs=([pl.BlockSpec((tm, D), lambda i: (i, 0))]
                  + [pl.BlockSpec((tm, a.shape[1]), lambda i: (i, 0)) for a in acts]
                  + [pl.BlockSpec(w.shape, lambda i: (0, 0)) for w in weights]),
        out_specs=pl.BlockSpec((tm, D), lambda i: (i, 0)),
        out_shape=jax.ShapeDtypeStruct((M, D), F32),
        compiler_params=_params("parallel"),
        name="outproj_residual",
    )(x, *acts, *weights)


def _mlp_kernel(x_ref, g_ref, w1_ref, w2_ref, fg_ref, o_ref, *, tf, final_norm):
    x = x_ref[...]
    hb = _rms(x, g_ref[...]).astype(BF16)
    acc = x
    for c in range(w1_ref.shape[1] // tf):
        u = jnp.maximum(_dot(hb, w1_ref[:, c * tf:(c + 1) * tf]), 0.0)
        acc = acc + _dot((u * u).astype(BF16), w2_ref[c * tf:(c + 1) * tf, :])
    if final_norm:
        acc = _rms(acc, fg_ref[...])
    o_ref[...] = acc


def _mlp(x, g, w1, w2, final_g, *, tm, tf, final_norm):
    M, D = x.shape
    const = lambda a: pl.BlockSpec(a.shape, lambda i: (0, 0), pipeline_mode=pl.Buffered(1))
    return pl.pallas_call(
        functools.partial(_mlp_kernel, tf=tf, final_norm=final_norm),
        grid=(M // tm,),
        in_specs=[pl.BlockSpec((tm, D), lambda i: (i, 0)), const(g), const(w1), const(w2),
                  const(final_g)],
        out_specs=pl.BlockSpec((tm, D), lambda i: (i, 0)),
        out_shape=jax.ShapeDtypeStruct((M, D), F32),
        compiler_params=_params("parallel"),
        name="mlp",
    )(x, g, w1, w2, final_g)


def _pad_cols(a, n):
    return jnp.pad(a, ((0, 0), (0, n - a.shape[1])))


def _tile(n, pref):
    return min(n, pref)


def kernel(x, norm_mix_g, norm_mlp_g, ab_w_in, ab_conv_w, ab_conv_b, ab_i_bias, ab_f_bias, ab_mnorm_g, ab_lam_q1, ab_lam_k1, ab_lam_q2, ab_lam_k2, ab_dnorm_g, ab_w_out, ssm_w_in, ssm_conv_w, ssm_conv_b, ssm_dt_bias, ssm_a_log, ssm_d, ssm_norm_g, ssm_w_out, mlp_w1, mlp_w2, final_g):
    B, T, D = x.shape
    M = B * T
    depth = norm_mix_g.shape[0]
    tm_proj = _tile(M, 1024)
    tm_mlp = _tile(M, 512)
    row = lambda a: a.reshape(1, -1).astype(F32)
    xf = x.reshape(M, D)

    for i in range(depth):
        j = i // 2
        if i % 2 == 0:
            w = ab_w_in[j]
            mw = 2 * MLSTM_HEADS * MLSTM_DH
            g0 = 2 * mw
            w_main = jnp.concatenate([w[:, :g0], w[:, g0 + 2 * MLSTM_HEADS:]], axis=1).astype(BF16)
            w_gate = _pad_cols(w[:, g0:g0 + 2 * MLSTM_HEADS], LANES).astype(BF16)
            proj, gates = _norm_matmul(xf, row(norm_mix_g[i]), w_main, w_gate, tm=tm_mlp, tn=512)
            proj = proj.reshape(B, T, -1)
            gates = gates.reshape(B, T, LANES)
            gate_bias = _pad_cols(jnp.concatenate([ab_i_bias[j], ab_f_bias[j]]).reshape(1, -1), LANES)
            hm = _mlstm(proj, gates, ab_conv_w[j], row(ab_conv_b[j]), gate_bias,
                        row(ab_mnorm_g[j]), L=_tile(T, 128))
            lam_init = 0.8 - 0.6 * math.exp(-0.3 * i)
            slopes = jnp.exp2(-8.0 * jnp.arange(1, DIFF_HEADS + 1, dtype=F32) / DIFF_HEADS)
            lam_params = jnp.stack([ab_lam_q1[j], ab_lam_k1[j], ab_lam_q2[j], ab_lam_k2[j]])
            qblk = g0 // LANES
            hd = _diff_attn(proj, slopes, lam_params, row(ab_dnorm_g[j]), tq=_tile(T, 512),
                            lam_init=lam_init, q0=qblk, k0=qblk + DIFF_HEADS, v0=qblk + 2 * DIFF_HEADS)
            wo = ab_w_out[j].astype(BF16)
            half = MLSTM_HEADS * MLSTM_DH
            xf = _proj_residual(xf, [hm.reshape(M, -1), hd.reshape(M, -1)], [wo[:half], wo[half:]],
                                tm=tm_proj)
        else:
            w = ssm_w_in[j]
            n_main = w.shape[1] - SSM_GROUPS * SSM_HPG
            w_main = w[:, :n_main].astype(BF16)
            w_dt = _pad_cols(w[:, n_main:], LANES).astype(BF16)
            proj, dtraw = _norm_matmul(xf, row(norm_mix_g[i]), w_main, w_dt, tm=tm_mlp, tn=512)
            head_params = jnp.pad(jnp.stack([ssm_dt_bias[j], ssm_a_log[j], ssm_d[j]]),
                                  ((0, HALO - 3), (0, LANES - SSM_GROUPS * SSM_HPG)))
            y = _ssd(proj.reshape(B, T, -1), dtraw.reshape(B, T, LANES), ssm_conv_w[j],
                     row(ssm_conv_b[j]), head_params, row(ssm_norm_g[j]), L=_tile(T, 256))
            xf = _proj_residual(xf, [y.reshape(M, -1)], [ssm_w_out[j].astype(BF16)], tm=tm_proj)
        xf = _mlp(xf, row(norm_mlp_g[i]), mlp_w1[i].astype(BF16), mlp_w2[i].astype(BF16),
                  row(final_g), tm=tm_mlp, tf=512, final_norm=(i == depth - 1))
    return xf.reshape(B, T, D)
```

```python
import functools
import math

import jax
import jax.numpy as jnp
from jax import lax
from jax.experimental import pallas as pl
from jax.experimental.pallas import tpu as pltpu

F32 = jnp.float32
BF16 = jnp.bfloat16
HIGHEST = lax.Precision.HIGHEST

RMS_EPS = 1e-5
LOG2E = 1.4426950408889634
CONV_WIDTH = 4
LANES = 128
HALO = 8
NEG_BIG = -1e30
ACC_PAD = 16
VMEM_LIMIT = 56 * 1024 * 1024

MLSTM_HEADS = 4
MLSTM_DH = 128
DIFF_HEADS = 4
DIFF_DH = 64
DIFF_DV = 128
SSM_GROUPS = 8
SSM_HPG = 4
SSM_HEADDIM = 64
SSM_STATE = 128
SSM_GW = SSM_HPG * SSM_HEADDIM


def _params(*sem, flags=None):
    return pltpu.CompilerParams(dimension_semantics=sem, vmem_limit_bytes=VMEM_LIMIT, flags=flags)


def _sigmoid(x):
    return 1.0 / (1.0 + jnp.exp2(x * -LOG2E))


def _rms(x, g):
    return x * lax.rsqrt(jnp.mean(x * x, axis=-1, keepdims=True) + RMS_EPS) * g


def _dot(a, b):
    return jnp.dot(a, b, preferred_element_type=F32)


def _dot_nt(a, b):
    return lax.dot_general(a, b, (((1,), (1,)), ((), ())), preferred_element_type=F32)


def _conv_silu(buf, x, prev, w, b):
    L = x.shape[0]
    buf[0:HALO, :] = prev
    buf[HALO:HALO + L, :] = x
    y = x * w[CONV_WIDTH - 1:CONV_WIDTH] + b
    for s in range(1, CONV_WIDTH):
        y = y + buf[HALO - s:HALO - s + L, :] * w[CONV_WIDTH - 1 - s:CONV_WIDTH - s]
    return y * _sigmoid(y)


def _norm_matmul_kernel(x_ref, g_ref, w_ref, wg_ref, cw_ref, cb_ref, o_ref, og_ref,
                        halo, buf0, buf1, *, tn, acts, order, conv0, tiles_per_seq):
    tm = x_ref.shape[0]

    @pl.when(pl.program_id(0) % tiles_per_seq == 0)
    def _():
        halo[...] = jnp.zeros_like(halo)

    hb = _rms(x_ref[...], g_ref[...]).astype(BF16)
    og_ref[...] = _dot(hb, wg_ref[...])
    bufs = [buf0, buf1]
    for c in order:
        kind, scale = acts[c]
        cols = slice(c * tn, (c + 1) * tn)
        acc = _dot(hb, w_ref[:, cols])
        if kind == "conv":
            cc = slice(c * tn - conv0, (c + 1) * tn - conv0)
            prev = halo[:, cc]
            halo[:, cc] = acc[tm - HALO:]
            acc = _conv_silu(bufs[0], acc, prev, cw_ref[:, cc], cb_ref[:, cc])
            bufs.reverse()
        elif kind == "silu":
            acc = acc * _sigmoid(acc)
        elif kind == "sigmoid":
            acc = _sigmoid(acc)
        if scale != 1.0:
            acc = acc * scale
        o_ref[:, cols] = acc.astype(o_ref.dtype)


def _norm_matmul(x, g, w, wg, conv_w, conv_b, *, tm, tn, acts, conv0, tiles_per_seq):
    M, D = x.shape
    N = w.shape[1]
    order = range(len(acts))
    const = lambda a: pl.BlockSpec(a.shape, lambda i: (0, 0), pipeline_mode=pl.Buffered(1))
    return pl.pallas_call(
        functools.partial(_norm_matmul_kernel, tn=tn, acts=acts, order=tuple(order), conv0=conv0,
                          tiles_per_seq=tiles_per_seq),
        grid=(M // tm,),
        in_specs=[pl.BlockSpec((tm, D), lambda i: (i, 0)), const(g), const(w), const(wg),
                  const(conv_w), const(conv_b)],
        out_specs=[
            pl.BlockSpec((tm, N), lambda i: (i, 0)),
            pl.BlockSpec((tm, LANES), lambda i: (i, 0)),
        ],
        out_shape=[jax.ShapeDtypeStruct((M, N), BF16), jax.ShapeDtypeStruct((M, LANES), F32)],
        scratch_shapes=[pltpu.VMEM((HALO, conv_w.shape[1]), F32),
                        pltpu.VMEM((HALO + tm, tn), F32),
                        pltpu.VMEM((HALO + tm, tn), F32)],
        compiler_params=_params("arbitrary"),
        name="norm_inproj",
    )(x, g, w, wg, conv_w, conv_b)


def _mlstm_kernel(q_ref, k_ref, v_ref, o_ref, gt_ref, gb_ref, ng_ref, out_ref,
                  c_st, n_st, m_st, *, L):
    H, DH = MLSTM_HEADS, MLSTM_DH

    @pl.when(pl.program_id(1) == 0)
    def _():
        c_st[...] = jnp.zeros_like(c_st)
        n_st[...] = jnp.zeros_like(n_st)
        m_st[...] = jnp.zeros_like(m_st)

    q = q_ref[0]
    k = k_ref[0]
    v = v_ref[0]
    og = o_ref[0].astype(F32)
    ng = ng_ref[...]

    gts = gt_ref[0] + gb_ref[...]
    lane = lax.broadcasted_iota(jnp.int32, (L, LANES), 1)
    logsig = jnp.minimum(gts, 0.0) - jnp.log1p(jnp.exp(-jnp.abs(gts)))
    G = jnp.where(lane < H, gts, logsig)
    GT = G.T
    row = lax.broadcasted_iota(jnp.int32, (L, L), 0)
    col = lax.broadcasted_iota(jnp.int32, (L, L), 1)
    causal = col <= row
    tri = causal.astype(F32)
    triT = (row <= col).astype(F32)
    Bc = jnp.dot(tri, G, precision=HIGHEST, preferred_element_type=F32)
    Br = jnp.dot(GT[0:2 * H], triT, precision=HIGHEST, preferred_element_type=F32)

    for h in range(H):
        sl = slice(h * DH, (h + 1) * DH)
        qb, kb, vh = q[:, sl], k[:, sl], v[:, sl]
        qh = qb.astype(F32)
        kh = kb.astype(F32)
        bc = Bc[:, H + h:H + h + 1]
        br = Br[H + h:H + h + 1, :]
        igr = GT[h:h + 1, :]
        igc = G[:, h:h + 1]
        m0 = m_st[h:h + 1, 0:1]
        c0 = c_st[h]
        n0 = n_st[h:h + 1, :]

        logw = jnp.where(causal, bc - br + igr, -jnp.inf)
        m_inter = bc + m0
        m_t = jnp.maximum(m_inter, jnp.max(logw, axis=-1, keepdims=True))
        wts = jnp.exp(logw - m_t) * _dot_nt(qb, kb)
        s_inter = jnp.exp(m_inter - m_t)
        num = _dot(wts.astype(BF16), vh) + s_inter * _dot(qb, c0.astype(BF16))
        den = (jnp.sum(wts, axis=-1, keepdims=True)
               + s_inter * jnp.sum(qh * n0, axis=-1, keepdims=True))
        hh = num / jnp.maximum(jnp.abs(den), jnp.exp(-m_t))

        b_last = bc[L - 1:L, :]
        m_loc = jnp.max(b_last - br + igr, axis=-1, keepdims=True)
        m_new = jnp.maximum(b_last + m0, m_loc)
        kw = kh * jnp.exp(b_last - bc + igc - m_new)
        sp = jnp.exp(b_last + m0 - m_new)
        c_st[h] = sp * c0 + _dot(kw.T.astype(BF16), vh)
        n_st[h:h + 1, :] = sp * n0 + jnp.sum(kw, axis=0, keepdims=True)
        m_st[h:h + 1, :] = jnp.broadcast_to(m_new, (1, LANES))

        out_ref[0, :, sl] = (og[:, sl] * _rms(hh, ng[:, sl])).astype(out_ref.dtype)


def _mlstm(proj, gates, gate_bias, norm_g, *, L):
    B, T, _ = proj.shape
    W = MLSTM_HEADS * MLSTM_DH
    blk = lambda c0: pl.BlockSpec((1, L, W), lambda b, c: (b, c, c0))
    full = lambda a: pl.BlockSpec(a.shape, lambda b, c: (0,) * a.ndim)
    return pl.pallas_call(
        functools.partial(_mlstm_kernel, L=L),
        grid=(B, T // L),
        in_specs=[blk(0), blk(1), blk(2), blk(3),
                  pl.BlockSpec((1, L, LANES), lambda b, c: (b, c, 0)),
                  full(gate_bias), full(norm_g)],
        out_specs=pl.BlockSpec((1, L, W), lambda b, c: (b, c, 0)),
        out_shape=jax.ShapeDtypeStruct((B, T, W), BF16),
        scratch_shapes=[pltpu.VMEM((MLSTM_HEADS, MLSTM_DH, MLSTM_DH), F32),
                        pltpu.VMEM((8, MLSTM_DH), F32),
                        pltpu.VMEM((8, LANES), F32)],
        compiler_params=_params("parallel", "arbitrary"),
        name="mlstm",
    )(proj, proj, proj, proj, gates, gate_bias, norm_g)


def _diff_attn_kernel(slopes_ref, lam_ref, q_ref, k_ref, v_ref, ng_ref, out_ref,
                      vT, ta, tb, acc1, acc2, m1, m2, *, tq, lam_init):
    h = pl.program_id(1)
    qi = pl.program_id(2)
    slope = slopes_ref[h]
    half, dv = DIFF_DH, DIFF_DV

    vT[qi, 0:dv, :] = v_ref[0, pl.ds(pl.multiple_of(qi * tq, tq), tq), :].T
    ones_row = lax.broadcasted_iota(jnp.int32, (ACC_PAD, tq), 0) == 0
    vT[qi, dv:dv + ACC_PAD, :] = jnp.where(ones_row, 1.0, 0.0).astype(BF16)

    lane = lax.broadcasted_iota(jnp.int32, (tq, LANES), 1)
    r = lax.broadcasted_iota(jnp.int32, (tq, LANES), 0)
    r_lo = (r & 255).astype(F32)
    r_hi = (r - (r & 255)).astype(F32)
    sv = jnp.full((tq, LANES), slope, F32)
    s_hi = sv.astype(BF16).astype(F32)
    s_mid = (sv - s_hi).astype(BF16).astype(F32)
    s_lo = (sv - s_hi - s_mid).astype(BF16).astype(F32)
    qs = q_ref[0] * (half ** -0.5)

    def operands(f0):
        fl = lane - f0
        kfeat = jnp.where((fl >= 0) & (fl < 3), r_lo, jnp.where((fl >= 3) & (fl < 6), r_hi, 0.0))
        third = jnp.where(fl >= 3, fl - 3, fl)
        qfeat = jnp.where(third == 0, s_hi, jnp.where(third == 1, s_mid, s_lo))
        qfeat = jnp.where((fl >= 0) & (fl < 6), qfeat, 0.0)
        keep = (lane < half) if f0 >= half else (lane >= half)
        return kfeat.astype(BF16), keep, jnp.where(keep, qs, qfeat.astype(BF16))

    ops = (operands(half), operands(0))
    maps = ((acc1, m1), (acc2, m2))
    for acc, m in maps:
        acc[...] = jnp.zeros_like(acc)
        m[...] = jnp.full_like(m, NEG_BIG)

    krow = lax.broadcasted_iota(jnp.int32, (tq, tq), 0)
    qcol = lax.broadcasted_iota(jnp.int32, (tq, tq), 1)

    def scores(j, mi):
        kb = k_ref[0, pl.ds(pl.multiple_of(j * tq, tq), tq), :]
        kfeat, keep, qop = ops[mi]
        return _dot_nt(jnp.where(keep, kb, kfeat), qop)

    def step(j, cur, nxt, masked):
        vTb = vT[j]
        cst = slope * (jnp.zeros((1, tq), jnp.int32) + (j - qi) * tq).astype(F32)
        for mi, (acc, m) in enumerate(maps):
            if nxt is not None:
                nxt[mi] = scores(j + 1, mi)
            t = cur[mi]
            if masked:
                t = jnp.where(krow <= qcol, t, NEG_BIG)
            m_old = m[...]
            m_new = jnp.maximum(m_old, jnp.max(t, axis=0, keepdims=True) + cst)
            p = jnp.exp(t - (m_new - cst)).astype(BF16)
            acc[...] = jnp.exp(m_old - m_new) * acc[...] + _dot(vTb, p)
            m[...] = m_new

    for mi in range(2):
        ta[mi] = scores(0, mi)

    def pair(i, carry):
        step(2 * i, ta, tb, False)
        step(2 * i + 1, tb, ta, False)
        return carry

    lax.fori_loop(0, lax.shift_right_logical(qi, 1), pair, 0)
    odd = (qi & 1) == 1

    @pl.when(odd)
    def _():
        step(qi - 1, ta, tb, False)
        step(qi, tb, None, True)

    @pl.when(jnp.logical_not(odd))
    def _():
        step(qi, ta, None, True)

    lp = lam_ref[...]
    lam = (jnp.exp(jnp.sum(lp[0:1] * lp[1:2], axis=-1, keepdims=True))
           - jnp.exp(jnp.sum(lp[2:3] * lp[3:4], axis=-1, keepdims=True)) + lam_init)
    a1 = acc1[...]
    a2 = acc2[...]
    o = (a1[:dv] / a1[dv:dv + 1] - lam * (a2[:dv] / a2[dv:dv + 1])).T
    out_ref[0] = (_rms(o, ng_ref[...]) * (1.0 - lam_init)).astype(out_ref.dtype)


def _diff_attn(proj, slopes, lam_params, norm_g, *, tq, lam_init, q0, k0, v0):
    B, T, _ = proj.shape
    H = DIFF_HEADS
    return pl.pallas_call(
        functools.partial(_diff_attn_kernel, tq=tq, lam_init=lam_init),
        grid=(B, H, T // tq),
        in_specs=[
            pl.BlockSpec(memory_space=pltpu.SMEM),
            pl.BlockSpec(lam_params.shape, lambda b, h, i: (0, 0)),
            pl.BlockSpec((1, tq, LANES), lambda b, h, i: (b, i, q0 + h)),
            pl.BlockSpec((1, T, LANES), lambda b, h, i: (b, 0, k0 + h)),
            pl.BlockSpec((1, T, LANES), lambda b, h, i: (b, 0, v0 + h)),
            pl.BlockSpec((1, LANES), lambda b, h, i: (0, h)),
        ],
        out_specs=pl.BlockSpec((1, tq, LANES), lambda b, h, i: (b, i, h)),
        out_shape=jax.ShapeDtypeStruct((B, T, H * DIFF_DV), BF16),
        scratch_shapes=[pltpu.VMEM((T // tq, DIFF_DV + ACC_PAD, tq), BF16),
                        pltpu.VMEM((2, tq, tq), F32), pltpu.VMEM((2, tq, tq), F32),
                        pltpu.VMEM((DIFF_DV + ACC_PAD, tq), F32),
                        pltpu.VMEM((DIFF_DV + ACC_PAD, tq), F32),
                        pltpu.VMEM((1, tq), F32), pltpu.VMEM((1, tq), F32)],
        compiler_params=_params("parallel", "parallel", "arbitrary"),
        name="diff_attn",
    )(slopes, lam_params, proj, proj, proj, norm_g)


def _ssd_kernel(z_ref, x_ref, b_ref, c_ref, dt_ref, hp_ref, ng_ref, out_ref,
                s_st, dt_s, acs_s, art_s, *, L):
    R, P, GW = SSM_HPG, SSM_HEADDIM, SSM_GW
    g = pl.program_id(2)

    @pl.when(pl.program_id(1) == 0)
    def _():
        s_st[g] = jnp.zeros(s_st.shape[1:], F32)

    row = lax.broadcasted_iota(jnp.int32, (L, L), 0)
    col = lax.broadcasted_iota(jnp.int32, (L, L), 1)
    causal = col <= row

    @pl.when(g == 0)
    def _():
        hp = hp_ref[...]
        pre = dt_ref[0] + hp[0:1]
        dt = jnp.maximum(pre, 0.0) + jnp.log1p(jnp.exp(-jnp.abs(pre)))
        da = dt * (-jnp.exp(hp[1:2]))
        acs = jnp.dot(causal.astype(F32), da, precision=HIGHEST, preferred_element_type=F32)
        dt_s[...] = dt
        acs_s[...] = acs
        art_s[...] = acs.T

    x = x_ref[0].astype(F32)
    bmb = b_ref[0]
    cmb = c_ref[0]

    shift = (LANES - R * g) & (LANES - 1)
    dtg = pltpu.roll(dt_s[...], shift, 1)
    acg = pltpu.roll(acs_s[...], shift, 1)
    dsk = pltpu.roll(hp_ref[...], shift, 1)[2:3]
    cols = jnp.concatenate([dtg, acg, jnp.broadcast_to(dsk, (HALO, LANES))], axis=0)
    hi = cols.astype(BF16)
    rest = cols - hi.astype(F32)
    mid = rest.astype(BF16)
    lo = (rest - mid.astype(F32)).astype(BF16)
    er = lax.broadcasted_iota(jnp.int32, (3 * LANES, GW), 0) & (LANES - 1)
    ec = lax.broadcasted_iota(jnp.int32, (3 * LANES, GW), 1)
    expand = ((ec >= er * P) & (ec < (er + 1) * P)).astype(BF16)
    spread = _dot(jnp.concatenate([hi, mid, lo], axis=1), expand)
    dt_f, ac_f, d_f = spread[:L], spread[L:2 * L], spread[2 * L:2 * L + 1]
    a_last = ac_f[L - 1:L, :]

    cb = _dot_nt(cmb, bmb)
    s0 = s_st[g]
    xdt = x * dt_f
    xdt_b = xdt.astype(BF16)
    y = jnp.exp(ac_f) * _dot(cmb, s0.astype(BF16)) + d_f * x
    lane = lax.broadcasted_iota(jnp.int32, (L, GW), 1)
    for r in range(R):
        ac = acg[:, r:r + 1]
        ar = art_s[pl.ds(g * R + r, 1), :]
        decay = jnp.exp(jnp.where(causal, ac - ar, -jnp.inf))
        own = (lane >= r * P) & (lane < (r + 1) * P)
        y = y + _dot((cb * decay).astype(BF16), jnp.where(own, xdt_b, jnp.zeros_like(xdt_b)))

    xw = (xdt * jnp.exp(a_last - ac_f)).astype(BF16)
    s_st[g] = jnp.exp(a_last) * s0 + _dot(bmb.T, xw)

    y = y * z_ref[0].astype(F32)
    out_ref[0] = _rms(y, ng_ref[...]).astype(out_ref.dtype)


def _ssd(proj, dtraw, head_params, norm_g, *, L):
    B, T, _ = proj.shape
    G, GW, N = SSM_GROUPS, SSM_GW, SSM_STATE
    inner = G * GW
    xo, bo, co = inner // GW, 2 * inner // N, (2 * inner + G * N) // N
    return pl.pallas_call(
        functools.partial(_ssd_kernel, L=L),
        grid=(B, T // L, G),
        in_specs=[
            pl.BlockSpec((1, L, GW), lambda b, c, g: (b, c, g)),
            pl.BlockSpec((1, L, GW), lambda b, c, g: (b, c, xo + g)),
            pl.BlockSpec((1, L, N), lambda b, c, g: (b, c, bo + g)),
            pl.BlockSpec((1, L, N), lambda b, c, g: (b, c, co + g)),
            pl.BlockSpec((1, L, LANES), lambda b, c, g: (b, c, 0)),
            pl.BlockSpec(head_params.shape, lambda b, c, g: (0, 0)),
            pl.BlockSpec((1, GW), lambda b, c, g: (0, g)),
        ],
        out_specs=pl.BlockSpec((1, L, GW), lambda b, c, g: (b, c, g)),
        out_shape=jax.ShapeDtypeStruct((B, T, inner), BF16),
        scratch_shapes=[pltpu.VMEM((G, N, GW), F32),
                        pltpu.VMEM((L, LANES), F32),
                        pltpu.VMEM((L, LANES), F32),
                        pltpu.VMEM((LANES, L), F32)],
        compiler_params=_params("parallel", "arbitrary", "arbitrary"),
        name="ssd",
    )(proj, proj, proj, proj, dtraw, head_params, norm_g)


def _proj_residual_kernel(*refs, n_in):
    x_ref = refs[0]
    a_refs = refs[1:1 + n_in]
    w_refs = refs[1 + n_in:1 + 2 * n_in]
    o_ref = refs[1 + 2 * n_in]
    acc = x_ref[...]
    for a_ref, w_ref in zip(a_refs, w_refs):
        acc = acc + _dot(a_ref[...], w_ref[...])
    o_ref[...] = acc


def _proj_residual(x, acts, weights, *, tm):
    M, D = x.shape
    n_in = len(acts)
    return pl.pallas_call(
        functools.partial(_proj_residual_kernel, n_in=n_in),
        grid=(M // tm,),
        in_specs=([pl.BlockSpec((tm, D), lambda i: (i, 0))]
                  + [pl.BlockSpec((tm, a.shape[1]), lambda i: (i, 0)) for a in acts]
                  + [pl.BlockSpec(w.shape, lambda i: (0, 0)) for w in weights]),
        out_specs=pl.BlockSpec((tm, D), lambda i: (i, 0)),
        out_shape=jax.ShapeDtypeStruct((M, D), F32),
        compiler_params=_params("parallel"),
        name="outproj_residual",
    )(x, *acts, *weights)


def _mlp_kernel(x_ref, g_ref, w1_ref, w2_ref, fg_ref, o_ref, *, tf, final_norm):
    x = x_ref[...]
    hb = _rms(x, g_ref[...]).astype(BF16)
    acc = x
    for c in range(w1_ref.shape[1] // tf):
        u = jnp.maximum(_dot(hb, w1_ref[:, c * tf:(c + 1) * tf]), 0.0)
        acc = acc + _dot((u * u).astype(BF16), w2_ref[c * tf:(c + 1) * tf, :])
    if final_norm:
        acc = _rms(acc, fg_ref[...])
    o_ref[...] = acc


def _mlp(x, g, w1, w2, final_g, *, tm, tf, final_norm):
    M, D = x.shape
    const = lambda a: pl.BlockSpec(a.shape, lambda i: (0, 0), pipeline_mode=pl.Buffered(1))
    return pl.pallas_call(
        functools.partial(_mlp_kernel, tf=tf, final_norm=final_norm),
        grid=(M // tm,),
        in_specs=[pl.BlockSpec((tm, D), lambda i: (i, 0)), const(g), const(w1), const(w2),
                  const(final_g)],
        out_specs=pl.BlockSpec((tm, D), lambda i: (i, 0)),
        out_shape=jax.ShapeDtypeStruct((M, D), F32),
        compiler_params=_params("parallel"),
        name="mlp",
    )(x, g, w1, w2, final_g)


def _pad_cols(a, n):
    return jnp.pad(a, ((0, 0), (0, n - a.shape[1])))


def _tile(n, pref):
    return min(n, pref)


def kernel(x, norm_mix_g, norm_mlp_g, ab_w_in, ab_conv_w, ab_conv_b, ab_i_bias, ab_f_bias, ab_mnorm_g, ab_lam_q1, ab_lam_k1, ab_lam_q2, ab_lam_k2, ab_dnorm_g, ab_w_out, ssm_w_in, ssm_conv_w, ssm_conv_b, ssm_dt_bias, ssm_a_log, ssm_d, ssm_norm_g, ssm_w_out, mlp_w1, mlp_w2, final_g):
    B, T, D = x.shape
    M = B * T
    depth = norm_mix_g.shape[0]
    tm_proj = _tile(M, 1024)
    tm_mlp = _tile(T, 512)
    tn = 512
    row = lambda a: a.reshape(1, -1).astype(F32)
    xf = x.reshape(M, D)

    for i in range(depth):
        j = i // 2
        if i % 2 == 0:
            w = ab_w_in[j]
            mw = MLSTM_HEADS * MLSTM_DH
            g0 = 4 * mw
            w_main = jnp.concatenate([w[:, :g0], w[:, g0 + 2 * MLSTM_HEADS:]], axis=1).astype(BF16)
            w_gate = _pad_cols(w[:, g0:g0 + 2 * MLSTM_HEADS], LANES).astype(BF16)
            acts = ((("conv", 1.0),) * (mw // tn) + (("conv", MLSTM_DH ** -0.5),) * (mw // tn)
                    + (("none", 1.0),) * (mw // tn) + (("sigmoid", 1.0),) * (mw // tn)
                    + (("none", 1.0),) * ((w_main.shape[1] - g0) // tn))
            proj, gates = _norm_matmul(xf, row(norm_mix_g[i]), w_main, w_gate, ab_conv_w[j],
                                       row(ab_conv_b[j]), tm=tm_mlp, tn=tn, acts=acts, conv0=0,
                                       tiles_per_seq=T // tm_mlp)
            proj = proj.reshape(B, T, -1)
            gates = gates.reshape(B, T, LANES)
            gate_bias = _pad_cols(jnp.concatenate([ab_i_bias[j], ab_f_bias[j]]).reshape(1, -1), LANES)
            hm = _mlstm(proj, gates, gate_bias, row(ab_mnorm_g[j]), L=_tile(T, 256))
            lam_init = 0.8 - 0.6 * math.exp(-0.3 * i)
            slopes = jnp.exp2(-8.0 * jnp.arange(1, DIFF_HEADS + 1, dtype=F32) / DIFF_HEADS)
            lam_params = jnp.stack([ab_lam_q1[j], ab_lam_k1[j], ab_lam_q2[j], ab_lam_k2[j]])
            qblk = g0 // LANES
            hd = _diff_attn(proj, slopes, lam_params, row(ab_dnorm_g[j]), tq=_tile(T, 512),
                            lam_init=lam_init, q0=qblk, k0=qblk + DIFF_HEADS, v0=qblk + 2 * DIFF_HEADS)
            wo = ab_w_out[j].astype(BF16)
            half = MLSTM_HEADS * MLSTM_DH
            xf = _proj_residual(xf, [hm.reshape(M, -1), hd.reshape(M, -1)], [wo[:half], wo[half:]],
                                tm=tm_proj)
        else:
            w = ssm_w_in[j]
            n_main = w.shape[1] - SSM_GROUPS * SSM_HPG
            w_main = w[:, :n_main].astype(BF16)
            w_dt = _pad_cols(w[:, n_main:], LANES).astype(BF16)
            inner = SSM_GROUPS * SSM_GW
            acts = (("silu", 1.0),) * (inner // tn) + (("conv", 1.0),) * ((n_main - inner) // tn)
            proj, dtraw = _norm_matmul(xf, row(norm_mix_g[i]), w_main, w_dt, ssm_conv_w[j],
                                       row(ssm_conv_b[j]), tm=tm_mlp, tn=tn, acts=acts, conv0=inner,
                                       tiles_per_seq=T // tm_mlp)
            head_params = jnp.pad(jnp.stack([ssm_dt_bias[j], ssm_a_log[j], ssm_d[j]]),
                                  ((0, HALO - 3), (0, LANES - SSM_GROUPS * SSM_HPG)))
            y = _ssd(proj.reshape(B, T, -1), dtraw.reshape(B, T, LANES), head_params,
                     row(ssm_norm_g[j]), L=_tile(T, 256))
            xf = _proj_residual(xf, [y.reshape(M, -1)], [ssm_w_out[j].astype(BF16)], tm=tm_proj)
        xf = _mlp(xf, row(norm_mlp_g[i]), mlp_w1[i].astype(BF16), mlp_w2[i].astype(BF16),
                  row(final_g), tm=tm_mlp, tf=512, final_norm=(i == depth - 1))
    return xf.reshape(B, T, D)
```

```python
import functools
import math

import jax
import jax.numpy as jnp
from jax import lax
from jax.experimental import pallas as pl
from jax.experimental.pallas import tpu as pltpu

F32 = jnp.float32
BF16 = jnp.bfloat16
HIGHEST = lax.Precision.HIGHEST

RMS_EPS = 1e-5
LOG2E = 1.4426950408889634
CONV_WIDTH = 4
LANES = 128
HALO = 8
NEG_BIG = -1e30
ACC_PAD = 16
VMEM_LIMIT = 56 * 1024 * 1024

MLSTM_HEADS = 4
MLSTM_DH = 128
DIFF_HEADS = 4
DIFF_DH = 64
DIFF_DV = 128
SSM_GROUPS = 8
SSM_HPG = 4
SSM_HEADDIM = 64
SSM_STATE = 128
SSM_GW = SSM_HPG * SSM_HEADDIM


def _params(*sem, flags=None):
    return pltpu.CompilerParams(dimension_semantics=sem, vmem_limit_bytes=VMEM_LIMIT, flags=flags)


def _sigmoid(x):
    return 1.0 / (1.0 + jnp.exp2(x * -LOG2E))


def _rms(x, g):
    return x * lax.rsqrt(jnp.mean(x * x, axis=-1, keepdims=True) + RMS_EPS) * g


def _dot(a, b):
    return jnp.dot(a, b, preferred_element_type=F32)


def _dot_nt(a, b):
    return lax.dot_general(a, b, (((1,), (1,)), ((), ())), preferred_element_type=F32)


def _conv_silu(buf, x, prev, w, b):
    L = x.shape[0]
    buf[0:HALO, :] = prev
    buf[HALO:HALO + L, :] = x
    y = x * w[CONV_WIDTH - 1:CONV_WIDTH] + b
    for s in range(1, CONV_WIDTH):
        y = y + buf[HALO - s:HALO - s + L, :] * w[CONV_WIDTH - 1 - s:CONV_WIDTH - s]
    return y * _sigmoid(y)


def _norm_matmul_kernel(x_ref, g_ref, w_ref, wg_ref, cw_ref, cb_ref, o_ref, og_ref,
                        halo, buf0, buf1, *, tn, acts, order, conv0, tiles_per_seq):
    tm = x_ref.shape[0]

    @pl.when(pl.program_id(0) % tiles_per_seq == 0)
    def _():
        halo[...] = jnp.zeros_like(halo)

    hb = _rms(x_ref[...], g_ref[...]).astype(BF16)
    og_ref[...] = _dot(hb, wg_ref[...])
    bufs = [buf0, buf1]

    def finish(c, acc):
        kind, scale = acts[c]
        cols = slice(c * tn, (c + 1) * tn)
        if kind == "conv":
            cc = slice(c * tn - conv0, (c + 1) * tn - conv0)
            prev = halo[:, cc]
            halo[:, cc] = acc[tm - HALO:]
            acc = _conv_silu(bufs[0], acc, prev, cw_ref[:, cc], cb_ref[:, cc])
            bufs.reverse()
        elif kind == "silu":
            acc = acc * _sigmoid(acc)
        elif kind == "sigmoid":
            acc = _sigmoid(acc)
        if scale != 1.0:
            acc = acc * scale
        o_ref[:, cols] = acc.astype(o_ref.dtype)

    pending = None
    for c in order:
        acc = _dot(hb, w_ref[:, c * tn:(c + 1) * tn])
        if pending is not None:
            finish(*pending)
        pending = (c, acc)
    finish(*pending)


def _norm_matmul(x, g, w, wg, conv_w, conv_b, *, tm, tn, acts, conv0, tiles_per_seq):
    M, D = x.shape
    N = w.shape[1]
    order = range(len(acts))
    const = lambda a: pl.BlockSpec(a.shape, lambda i: (0, 0), pipeline_mode=pl.Buffered(1))
    return pl.pallas_call(
        functools.partial(_norm_matmul_kernel, tn=tn, acts=acts, order=tuple(order), conv0=conv0,
                          tiles_per_seq=tiles_per_seq),
        grid=(M // tm,),
        in_specs=[pl.BlockSpec((tm, D), lambda i: (i, 0)), const(g), const(w), const(wg),
                  const(conv_w), const(conv_b)],
        out_specs=[
            pl.BlockSpec((tm, N), lambda i: (i, 0)),
            pl.BlockSpec((tm, LANES), lambda i: (i, 0)),
        ],
        out_shape=[jax.ShapeDtypeStruct((M, N), BF16), jax.ShapeDtypeStruct((M, LANES), F32)],
        scratch_shapes=[pltpu.VMEM((HALO, conv_w.shape[1]), F32),
                        pltpu.VMEM((HALO + tm, tn), F32),
                        pltpu.VMEM((HALO + tm, tn), F32)],
        compiler_params=_params("arbitrary"),
        name="norm_inproj",
    )(x, g, w, wg, conv_w, conv_b)


def _mlstm_kernel(q_ref, k_ref, v_ref, o_ref, gt_ref, gb_ref, ng_ref, out_ref,
                  c_st, n_st, m_st, *, L):
    H, DH = MLSTM_HEADS, MLSTM_DH

    @pl.when(pl.program_id(1) == 0)
    def _():
        c_st[...] = jnp.zeros_like(c_st)
        n_st[...] = jnp.zeros_like(n_st)
        m_st[...] = jnp.zeros_like(m_st)

    q = q_ref[0]
    k = k_ref[0]
    v = v_ref[0]
    og = o_ref[0].astype(F32)
    ng = ng_ref[...]

    gts = gt_ref[0] + gb_ref[...]
    lane = lax.broadcasted_iota(jnp.int32, (L, LANES), 1)
    logsig = jnp.minimum(gts, 0.0) - jnp.log1p(jnp.exp(-jnp.abs(gts)))
    G = jnp.where(lane < H, gts, logsig)
    GT = G.T
    row = lax.broadcasted_iota(jnp.int32, (L, L), 0)
    col = lax.broadcasted_iota(jnp.int32, (L, L), 1)
    causal = col <= row
    tri = causal.astype(F32)
    triT = (row <= col).astype(F32)
    Bc = jnp.dot(tri, G, precision=HIGHEST, preferred_element_type=F32)
    Br = jnp.dot(GT[0:2 * H], triT, precision=HIGHEST, preferred_element_type=F32)

    for h in range(H):
        sl = slice(h * DH, (h + 1) * DH)
        qb, kb, vh = q[:, sl], k[:, sl], v[:, sl]
        qh = qb.astype(F32)
        kh = kb.astype(F32)
        bc = Bc[:, H + h:H + h + 1]
        br = Br[H + h:H + h + 1, :]
        igr = GT[h:h + 1, :]
        igc = G[:, h:h + 1]
        m0 = m_st[h:h + 1, 0:1]
        c0 = c_st[h]
        n0 = n_st[h:h + 1, :]

        logw = jnp.where(causal, bc - br + igr, -jnp.inf)
        m_inter = bc + m0
        m_t = jnp.maximum(m_inter, jnp.max(logw, axis=-1, keepdims=True))
        wts = jnp.exp(logw - m_t) * _dot_nt(qb, kb)
        s_inter = jnp.exp(m_inter - m_t)
        num = _dot(wts.astype(BF16), vh) + s_inter * _dot(qb, c0.astype(BF16))
        den = (jnp.sum(wts, axis=-1, keepdims=True)
               + s_inter * jnp.sum(qh * n0, axis=-1, keepdims=True))
        hh = num / jnp.maximum(jnp.abs(den), jnp.exp(-m_t))

        b_last = bc[L - 1:L, :]
        m_loc = jnp.max(b_last - br + igr, axis=-1, keepdims=True)
        m_new = jnp.maximum(b_last + m0, m_loc)
        kw = kh * jnp.exp(b_last - bc + igc - m_new)
        sp = jnp.exp(b_last + m0 - m_new)
        c_st[h] = sp * c0 + _dot(kw.T.astype(BF16), vh)
        n_st[h:h + 1, :] = sp * n0 + jnp.sum(kw, axis=0, keepdims=True)
        m_st[h:h + 1, :] = jnp.broadcast_to(m_new, (1, LANES))

        out_ref[0, :, sl] = (og[:, sl] * _rms(hh, ng[:, sl])).astype(out_ref.dtype)


def _mlstm(proj, gates, gate_bias, norm_g, *, L):
    B, T, _ = proj.shape
    W = MLSTM_HEADS * MLSTM_DH
    blk = lambda c0: pl.BlockSpec((1, L, W), lambda b, c: (b, c, c0))
    full = lambda a: pl.BlockSpec(a.shape, lambda b, c: (0,) * a.ndim)
    return pl.pallas_call(
        functools.partial(_mlstm_kernel, L=L),
        grid=(B, T // L),
        in_specs=[blk(0), blk(1), blk(2), blk(3),
                  pl.BlockSpec((1, L, LANES), lambda b, c: (b, c, 0)),
                  full(gate_bias), full(norm_g)],
        out_specs=pl.BlockSpec((1, L, W), lambda b, c: (b, c, 0)),
        out_shape=jax.ShapeDtypeStruct((B, T, W), BF16),
        scratch_shapes=[pltpu.VMEM((MLSTM_HEADS, MLSTM_DH, MLSTM_DH), F32),
                        pltpu.VMEM((8, MLSTM_DH), F32),
                        pltpu.VMEM((8, LANES), F32)],
        compiler_params=_params("parallel", "arbitrary"),
        name="mlstm",
    )(proj, proj, proj, proj, gates, gate_bias, norm_g)


def _diff_attn_kernel(slopes_ref, lam_ref, q_ref, k_ref, v_ref, ng_ref, out_ref,
                      vT, ta, tb, acc1, acc2, m1, m2, *, tq, lam_init):
    h = pl.program_id(1)
    qi = pl.program_id(2)
    slope = slopes_ref[h]
    half, dv = DIFF_DH, DIFF_DV

    vT[qi, 0:dv, :] = v_ref[0, pl.ds(pl.multiple_of(qi * tq, tq), tq), :].T
    ones_row = lax.broadcasted_iota(jnp.int32, (ACC_PAD, tq), 0) == 0
    vT[qi, dv:dv + ACC_PAD, :] = jnp.where(ones_row, 1.0, 0.0).astype(BF16)

    lane = lax.broadcasted_iota(jnp.int32, (tq, LANES), 1)
    r = lax.broadcasted_iota(jnp.int32, (tq, LANES), 0)
    r_lo = (r & 255).astype(F32)
    r_hi = (r - (r & 255)).astype(F32)
    sv = jnp.full((tq, LANES), slope, F32)
    s_hi = sv.astype(BF16).astype(F32)
    s_mid = (sv - s_hi).astype(BF16).astype(F32)
    s_lo = (sv - s_hi - s_mid).astype(BF16).astype(F32)
    qs = q_ref[0] * (half ** -0.5)

    def operands(f0):
        fl = lane - f0
        kfeat = jnp.where((fl >= 0) & (fl < 3), r_lo, jnp.where((fl >= 3) & (fl < 6), r_hi, 0.0))
        third = jnp.where(fl >= 3, fl - 3, fl)
        qfeat = jnp.where(third == 0, s_hi, jnp.where(third == 1, s_mid, s_lo))
        qfeat = jnp.where((fl >= 0) & (fl < 6), qfeat, 0.0)
        keep = (lane < half) if f0 >= half else (lane >= half)
        return kfeat.astype(BF16), keep, jnp.where(keep, qs, qfeat.astype(BF16))

    ops = (operands(half), operands(0))
    maps = ((acc1, m1), (acc2, m2))
    for acc, m in maps:
        acc[...] = jnp.zeros_like(acc)
        m[...] = jnp.full_like(m, NEG_BIG)

    krow = lax.broadcasted_iota(jnp.int32, (tq, tq), 0)
    qcol = lax.broadcasted_iota(jnp.int32, (tq, tq), 1)

    def scores(j, mi):
        kb = k_ref[0, pl.ds(pl.multiple_of(j * tq, tq), tq), :]
        kfeat, keep, qop = ops[mi]
        return _dot_nt(jnp.where(keep, kb, kfeat), qop)

    def step(j, cur, nxt, masked):
        vTb = vT[j]
        cst = slope * (jnp.zeros((1, tq), jnp.int32) + (j - qi) * tq).astype(F32)
        for mi, (acc, m) in enumerate(maps):
            if nxt is not None:
                nxt[mi] = scores(j + 1, mi)
            t = cur[mi]
            if masked:
                t = jnp.where(krow <= qcol, t, NEG_BIG)
            m_old = m[...]
            m_new = jnp.maximum(m_old, jnp.max(t, axis=0, keepdims=True) + cst)
            p = jnp.exp(t - (m_new - cst)).astype(BF16)
            acc[...] = jnp.exp(m_old - m_new) * acc[...] + _dot(vTb, p)
            m[...] = m_new

    for mi in range(2):
        ta[mi] = scores(0, mi)

    def pair(i, carry):
        step(2 * i, ta, tb, False)
        step(2 * i + 1, tb, ta, False)
        return carry

    lax.fori_loop(0, lax.shift_right_logical(qi, 1), pair, 0)
    odd = (qi & 1) == 1

    @pl.when(odd)
    def _():
        step(qi - 1, ta, tb, False)
        step(qi, tb, None, True)

    @pl.when(jnp.logical_not(odd))
    def _():
        step(qi, ta, None, True)

    lp = lam_ref[...]
    lam = (jnp.exp(jnp.sum(lp[0:1] * lp[1:2], axis=-1, keepdims=True))
           - jnp.exp(jnp.sum(lp[2:3] * lp[3:4], axis=-1, keepdims=True)) + lam_init)
    a1 = acc1[...]
    a2 = acc2[...]
    o = (a1[:dv] / a1[dv:dv + 1] - lam * (a2[:dv] / a2[dv:dv + 1])).T
    out_ref[0] = (_rms(o, ng_ref[...]) * (1.0 - lam_init)).astype(out_ref.dtype)


def _diff_attn(proj, slopes, lam_params, norm_g, *, tq, lam_init, q0, k0, v0):
    B, T, _ = proj.shape
    H = DIFF_HEADS
    return pl.pallas_call(
        functools.partial(_diff_attn_kernel, tq=tq, lam_init=lam_init),
        grid=(B, H, T // tq),
        in_specs=[
            pl.BlockSpec(memory_space=pltpu.SMEM),
            pl.BlockSpec(lam_params.shape, lambda b, h, i: (0, 0)),
            pl.BlockSpec((1, tq, LANES), lambda b, h, i: (b, i, q0 + h)),
            pl.BlockSpec((1, T, LANES), lambda b, h, i: (b, 0, k0 + h)),
            pl.BlockSpec((1, T, LANES), lambda b, h, i: (b, 0, v0 + h)),
            pl.BlockSpec((1, LANES), lambda b, h, i: (0, h)),
        ],
        out_specs=pl.BlockSpec((1, tq, LANES), lambda b, h, i: (b, i, h)),
        out_shape=jax.ShapeDtypeStruct((B, T, H * DIFF_DV), BF16),
        scratch_shapes=[pltpu.VMEM((T // tq, DIFF_DV + ACC_PAD, tq), BF16),
                        pltpu.VMEM((2, tq, tq), F32), pltpu.VMEM((2, tq, tq), F32),
                        pltpu.VMEM((DIFF_DV + ACC_PAD, tq), F32),
                        pltpu.VMEM((DIFF_DV + ACC_PAD, tq), F32),
                        pltpu.VMEM((1, tq), F32), pltpu.VMEM((1, tq), F32)],
        compiler_params=_params("parallel", "parallel", "arbitrary"),
        name="diff_attn",
    )(slopes, lam_params, proj, proj, proj, norm_g)


def _ssd_kernel(z_ref, x_ref, b_ref, c_ref, dt_ref, hp_ref, ng_ref, out_ref,
                s_st, dt_s, acs_s, art_s, *, L, gps):
    pg = pl.program_id(2)

    @pl.when(pl.program_id(1) == 0)
    def _():
        for k in range(gps):
            s_st[pg * gps + k] = jnp.zeros(s_st.shape[1:], F32)

    row = lax.broadcasted_iota(jnp.int32, (L, L), 0)
    col = lax.broadcasted_iota(jnp.int32, (L, L), 1)
    causal = col <= row

    @pl.when(pg == 0)
    def _():
        hp = hp_ref[...]
        pre = dt_ref[0] + hp[0:1]
        dt = jnp.maximum(pre, 0.0) + jnp.log1p(jnp.exp(-jnp.abs(pre)))
        da = dt * (-jnp.exp(hp[1:2]))
        acs = jnp.dot(causal.astype(F32), da, precision=HIGHEST, preferred_element_type=F32)
        dt_s[...] = dt
        acs_s[...] = acs
        art_s[...] = acs.T

    for k in range(gps):
        gw = slice(k * SSM_GW, (k + 1) * SSM_GW)
        gn = slice(k * SSM_STATE, (k + 1) * SSM_STATE)
        y = _ssd_group(pg * gps + k, x_ref[0, :, gw].astype(F32), b_ref[0, :, gn], c_ref[0, :, gn],
                       hp_ref, s_st, dt_s, acs_s, art_s, causal, L)
        y = y * z_ref[0, :, gw].astype(F32)
        out_ref[0, :, gw] = _rms(y, ng_ref[:, gw]).astype(out_ref.dtype)


def _ssd_group(g, x, bmb, cmb, hp_ref, s_st, dt_s, acs_s, art_s, causal, L):
    R, P, GW = SSM_HPG, SSM_HEADDIM, SSM_GW

    shift = (LANES - R * g) & (LANES - 1)
    dtg = pltpu.roll(dt_s[...], shift, 1)
    acg = pltpu.roll(acs_s[...], shift, 1)
    dsk = pltpu.roll(hp_ref[...], shift, 1)[2:3]
    cols = jnp.concatenate([dtg, acg, jnp.broadcast_to(dsk, (HALO, LANES))], axis=0)
    hi = cols.astype(BF16)
    rest = cols - hi.astype(F32)
    mid = rest.astype(BF16)
    lo = (rest - mid.astype(F32)).astype(BF16)
    er = lax.broadcasted_iota(jnp.int32, (3 * LANES, GW), 0) & (LANES - 1)
    ec = lax.broadcasted_iota(jnp.int32, (3 * LANES, GW), 1)
    expand = ((ec >= er * P) & (ec < (er + 1) * P)).astype(BF16)
    spread = _dot(jnp.concatenate([hi, mid, lo], axis=1), expand)
    dt_f, ac_f, d_f = spread[:L], spread[L:2 * L], spread[2 * L:2 * L + 1]
    a_last = ac_f[L - 1:L, :]

    cb = _dot_nt(cmb, bmb)
    s0 = s_st[g]
    xdt = x * dt_f
    xdt_b = xdt.astype(BF16)
    y = jnp.exp(ac_f) * _dot(cmb, s0.astype(BF16)) + d_f * x
    lane = lax.broadcasted_iota(jnp.int32, (L, GW), 1)
    for r in range(R):
        ac = acg[:, r:r + 1]
        ar = art_s[pl.ds(g * R + r, 1), :]
        decay = jnp.exp(jnp.where(causal, ac - ar, -jnp.inf))
        own = (lane >= r * P) & (lane < (r + 1) * P)
        y = y + _dot((cb * decay).astype(BF16), jnp.where(own, xdt_b, jnp.zeros_like(xdt_b)))

    xw = (xdt * jnp.exp(a_last - ac_f)).astype(BF16)
    s_st[g] = jnp.exp(a_last) * s0 + _dot(bmb.T, xw)
    return y


def _ssd(proj, dtraw, head_params, norm_g, *, L, gps):
    B, T, _ = proj.shape
    G = SSM_GROUPS
    GW, N = gps * SSM_GW, gps * SSM_STATE
    inner = SSM_GROUPS * SSM_GW
    xo, bo, co = inner // GW, 2 * inner // N, (2 * inner + G * SSM_STATE) // N
    return pl.pallas_call(
        functools.partial(_ssd_kernel, L=L, gps=gps),
        grid=(B, T // L, G // gps),
        in_specs=[
            pl.BlockSpec((1, L, GW), lambda b, c, g: (b, c, g)),
            pl.BlockSpec((1, L, GW), lambda b, c, g: (b, c, xo + g)),
            pl.BlockSpec((1, L, N), lambda b, c, g: (b, c, bo + g)),
            pl.BlockSpec((1, L, N), lambda b, c, g: (b, c, co + g)),
            pl.BlockSpec((1, L, LANES), lambda b, c, g: (b, c, 0)),
            pl.BlockSpec(head_params.shape, lambda b, c, g: (0, 0)),
            pl.BlockSpec((1, GW), lambda b, c, g: (0, g)),
        ],
        out_specs=pl.BlockSpec((1, L, GW), lambda b, c, g: (b, c, g)),
        out_shape=jax.ShapeDtypeStruct((B, T, inner), BF16),
        scratch_shapes=[pltpu.VMEM((G, SSM_STATE, SSM_GW), F32),
                        pltpu.VMEM((L, LANES), F32),
                        pltpu.VMEM((L, LANES), F32),
                        pltpu.VMEM((LANES, L), F32)],
        compiler_params=_params("parallel", "arbitrary", "arbitrary"),
        name="ssd",
    )(proj, proj, proj, proj, dtraw, head_params, norm_g)


def _proj_residual_kernel(*refs, n_in):
    x_ref = refs[0]
    a_refs = refs[1:1 + n_in]
    w_refs = refs[1 + n_in:1 + 2 * n_in]
    o_ref = refs[1 + 2 * n_in]
    acc = x_ref[...]
    for a_ref, w_ref in zip(a_refs, w_refs):
        acc = acc + _dot(a_ref[...], w_ref[...])
    o_ref[...] = acc


def _proj_residual(x, acts, weights, *, tm):
    M, D = x.shape
    n_in = len(acts)
    return pl.pallas_call(
        functools.partial(_proj_residual_kernel, n_in=n_in),
        grid=(M // tm,),
        in_specs=([pl.BlockSpec((tm, D), lambda i: (i, 0))]
                  + [pl.BlockSpec((tm, a.shape[1]), lambda i: (i, 0)) for a in acts]
                  + [pl.BlockSpec(w.shape, lambda i: (0, 0)) for w in weights]),
        out_specs=pl.BlockSpec((tm, D), lambda i: (i, 0)),
        out_shape=jax.ShapeDtypeStruct((M, D), F32),
        compiler_params=_params("parallel"),
        name="outproj_residual",
    )(x, *acts, *weights)


def _mlp_kernel(x_ref, g_ref, w1_ref, w2_ref, fg_ref, o_ref, *, tf, final_norm):
    x = x_ref[...]
    hb = _rms(x, g_ref[...]).astype(BF16)
    acc = x
    for c in range(w1_ref.shape[1] // tf):
        u = jnp.maximum(_dot(hb, w1_ref[:, c * tf:(c + 1) * tf]), 0.0)
        acc = acc + _dot((u * u).astype(BF16), w2_ref[c * tf:(c + 1) * tf, :])
    if final_norm:
        acc = _rms(acc, fg_ref[...])
    o_ref[...] = acc


def _mlp(x, g, w1, w2, final_g, *, tm, tf, final_norm):
    M, D = x.shape
    const = lambda a: pl.BlockSpec(a.shape, lambda i: (0, 0), pipeline_mode=pl.Buffered(1))
    return pl.pallas_call(
        functools.partial(_mlp_kernel, tf=tf, final_norm=final_norm),
        grid=(M // tm,),
        in_specs=[pl.BlockSpec((tm, D), lambda i: (i, 0)), const(g), const(w1), const(w2),
                  const(final_g)],
        out_specs=pl.BlockSpec((tm, D), lambda i: (i, 0)),
        out_shape=jax.ShapeDtypeStruct((M, D), F32),
        compiler_params=_params("parallel"),
        name="mlp",
    )(x, g, w1, w2, final_g)


def _pad_cols(a, n):
    return jnp.pad(a, ((0, 0), (0, n - a.shape[1])))


def _tile(n, pref):
    return min(n, pref)


def kernel(x, norm_mix_g, norm_mlp_g, ab_w_in, ab_conv_w, ab_conv_b, ab_i_bias, ab_f_bias, ab_mnorm_g, ab_lam_q1, ab_lam_k1, ab_lam_q2, ab_lam_k2, ab_dnorm_g, ab_w_out, ssm_w_in, ssm_conv_w, ssm_conv_b, ssm_dt_bias, ssm_a_log, ssm_d, ssm_norm_g, ssm_w_out, mlp_w1, mlp_w2, final_g):
    B, T, D = x.shape
    M = B * T
    depth = norm_mix_g.shape[0]
    tm_proj = _tile(M, 1024)
    tm_mlp = _tile(T, 512)
    tn = 512
    tm_in = _tile(T, 512)
    row = lambda a: a.reshape(1, -1).astype(F32)
    xf = x.reshape(M, D)

    for i in range(depth):
        j = i // 2
        if i % 2 == 0:
            w = ab_w_in[j]
            mw = MLSTM_HEADS * MLSTM_DH
            g0 = 4 * mw
            w_main = jnp.concatenate([w[:, :g0], w[:, g0 + 2 * MLSTM_HEADS:]], axis=1).astype(BF16)
            w_gate = _pad_cols(w[:, g0:g0 + 2 * MLSTM_HEADS], LANES).astype(BF16)
            acts = ((("conv", 1.0),) * (mw // tn) + (("conv", MLSTM_DH ** -0.5),) * (mw // tn)
                    + (("none", 1.0),) * (mw // tn) + (("sigmoid", 1.0),) * (mw // tn)
                    + (("none", 1.0),) * ((w_main.shape[1] - g0) // tn))
            proj, gates = _norm_matmul(xf, row(norm_mix_g[i]), w_main, w_gate, ab_conv_w[j],
                                       row(ab_conv_b[j]), tm=tm_in, tn=tn, acts=acts, conv0=0,
                                       tiles_per_seq=T // tm_in)
            proj = proj.reshape(B, T, -1)
            gates = gates.reshape(B, T, LANES)
            gate_bias = _pad_cols(jnp.concatenate([ab_i_bias[j], ab_f_bias[j]]).reshape(1, -1), LANES)
            hm = _mlstm(proj, gates, gate_bias, row(ab_mnorm_g[j]), L=_tile(T, 256))
            lam_init = 0.8 - 0.6 * math.exp(-0.3 * i)
            slopes = jnp.exp2(-8.0 * jnp.arange(1, DIFF_HEADS + 1, dtype=F32) / DIFF_HEADS)
            lam_params = jnp.stack([ab_lam_q1[j], ab_lam_k1[j], ab_lam_q2[j], ab_lam_k2[j]])
            qblk = g0 // LANES
            hd = _diff_attn(proj, slopes, lam_params, row(ab_dnorm_g[j]), tq=_tile(T, 512),
                            lam_init=lam_init, q0=qblk, k0=qblk + DIFF_HEADS, v0=qblk + 2 * DIFF_HEADS)
            wo = ab_w_out[j].astype(BF16)
            half = MLSTM_HEADS * MLSTM_DH
            xf = _proj_residual(xf, [hm.reshape(M, -1), hd.reshape(M, -1)], [wo[:half], wo[half:]],
                                tm=tm_proj)
        else:
            w = ssm_w_in[j]
            n_main = w.shape[1] - SSM_GROUPS * SSM_HPG
            w_main = w[:, :n_main].astype(BF16)
            w_dt = _pad_cols(w[:, n_main:], LANES).astype(BF16)
            inner = SSM_GROUPS * SSM_GW
            acts = (("silu", 1.0),) * (inner // tn) + (("conv", 1.0),) * ((n_main - inner) // tn)
            proj, dtraw = _norm_matmul(xf, row(norm_mix_g[i]), w_main, w_dt, ssm_conv_w[j],
                                       row(ssm_conv_b[j]), tm=tm_in, tn=tn, acts=acts, conv0=inner,
                                       tiles_per_seq=T // tm_in)
            head_params = jnp.pad(jnp.stack([ssm_dt_bias[j], ssm_a_log[j], ssm_d[j]]),
                                  ((0, HALO - 3), (0, LANES - SSM_GROUPS * SSM_HPG)))
            y = _ssd(proj.reshape(B, T, -1), dtraw.reshape(B, T, LANES), head_params,
                     row(ssm_norm_g[j]), L=_tile(T, 256), gps=8)
            xf = _proj_residual(xf, [y.reshape(M, -1)], [ssm_w_out[j].astype(BF16)], tm=tm_proj)
        xf = _mlp(xf, row(norm_mlp_g[i]), mlp_w1[i].astype(BF16), mlp_w2[i].astype(BF16),
                  row(final_g), tm=tm_mlp, tf=512, final_norm=(i == depth - 1))
    return xf.reshape(B, T, D)
```

```python
import functools
import math

import jax
import jax.numpy as jnp
from jax import lax
from jax.experimental import pallas as pl
from jax.experimental.pallas import tpu as pltpu

F32 = jnp.float32
BF16 = jnp.bfloat16
HIGHEST = lax.Precision.HIGHEST

RMS_EPS = 1e-5
LOG2E = 1.4426950408889634
CONV_WIDTH = 4
LANES = 128
HALO = 8
NEG_BIG = -1e30
ACC_PAD = 16
VMEM_LIMIT = 56 * 1024 * 1024

MLSTM_HEADS = 4
MLSTM_DH = 128
DIFF_HEADS = 4
DIFF_DH = 64
DIFF_DV = 128
SSM_GROUPS = 8
SSM_HPG = 4
SSM_HEADDIM = 64
SSM_STATE = 128
SSM_GW = SSM_HPG * SSM_HEADDIM


def _params(*sem, flags=None):
    return pltpu.CompilerParams(dimension_semantics=sem, vmem_limit_bytes=VMEM_LIMIT, flags=flags)


def _sigmoid(x):
    return 1.0 / (1.0 + jnp.exp2(x * -LOG2E))


def _rms(x, g):
    return x * lax.rsqrt(jnp.mean(x * x, axis=-1, keepdims=True) + RMS_EPS) * g


def _dot(a, b):
    return jnp.dot(a, b, preferred_element_type=F32)


def _dot_nt(a, b):
    return lax.dot_general(a, b, (((1,), (1,)), ((), ())), preferred_element_type=F32)


def _conv_silu(buf, x, prev, w, b):
    L = x.shape[0]
    buf[0:HALO, :] = prev
    buf[HALO:HALO + L, :] = x
    y = x * w[CONV_WIDTH - 1:CONV_WIDTH] + b
    for s in range(1, CONV_WIDTH):
        y = y + buf[HALO - s:HALO - s + L, :] * w[CONV_WIDTH - 1 - s:CONV_WIDTH - s]
    return y * _sigmoid(y)


def _norm_matmul_kernel(x_ref, g_ref, w_ref, wg_ref, cw_ref, cb_ref, o_ref, og_ref,
                        halo, buf0, buf1, *, tn, acts, order, conv0, tiles_per_seq):
    tm = x_ref.shape[0]

    @pl.when(pl.program_id(0) % tiles_per_seq == 0)
    def _():
        halo[...] = jnp.zeros_like(halo)

    hb = _rms(x_ref[...], g_ref[...]).astype(BF16)
    og_ref[...] = _dot(hb, wg_ref[...])
    bufs = [buf0, buf1]

    def finish(c, acc):
        kind, scale = acts[c]
        cols = slice(c * tn, (c + 1) * tn)
        if kind == "conv":
            cc = slice(c * tn - conv0, (c + 1) * tn - conv0)
            prev = halo[:, cc]
            halo[:, cc] = acc[tm - HALO:]
            acc = _conv_silu(bufs[0], acc, prev, cw_ref[:, cc], cb_ref[:, cc])
            bufs.reverse()
        elif kind == "silu":
            acc = acc * _sigmoid(acc)
        elif kind == "sigmoid":
            acc = _sigmoid(acc)
        if scale != 1.0:
            acc = acc * scale
        o_ref[:, cols] = acc.astype(o_ref.dtype)

    pending = None
    for c in order:
        acc = _dot(hb, w_ref[:, c * tn:(c + 1) * tn])
        if pending is not None:
            finish(*pending)
        pending = (c, acc)
    finish(*pending)


def _norm_matmul(x, g, w, wg, conv_w, conv_b, *, tm, tn, acts, conv0, tiles_per_seq):
    M, D = x.shape
    N = w.shape[1]
    order = range(len(acts))
    const = lambda a: pl.BlockSpec(a.shape, lambda i: (0, 0), pipeline_mode=pl.Buffered(1))
    return pl.pallas_call(
        functools.partial(_norm_matmul_kernel, tn=tn, acts=acts, order=tuple(order), conv0=conv0,
                          tiles_per_seq=tiles_per_seq),
        grid=(M // tm,),
        in_specs=[pl.BlockSpec((tm, D), lambda i: (i, 0)), const(g), const(w), const(wg),
                  const(conv_w), const(conv_b)],
        out_specs=[
            pl.BlockSpec((tm, N), lambda i: (i, 0)),
            pl.BlockSpec((tm, LANES), lambda i: (i, 0)),
        ],
        out_shape=[jax.ShapeDtypeStruct((M, N), BF16), jax.ShapeDtypeStruct((M, LANES), F32)],
        scratch_shapes=[pltpu.VMEM((HALO, conv_w.shape[1]), F32),
                        pltpu.VMEM((HALO + tm, tn), F32),
                        pltpu.VMEM((HALO + tm, tn), F32)],
        compiler_params=_params("arbitrary"),
        name="norm_inproj",
    )(x, g, w, wg, conv_w, conv_b)


def _mlstm_kernel(q_ref, k_ref, v_ref, o_ref, gt_ref, gb_ref, ng_ref, out_ref,
                  c_st, n_st, m_st, *, L):
    H, DH = MLSTM_HEADS, MLSTM_DH

    @pl.when(pl.program_id(1) == 0)
    def _():
        c_st[...] = jnp.zeros_like(c_st)
        n_st[...] = jnp.zeros_like(n_st)
        m_st[...] = jnp.zeros_like(m_st)

    q = q_ref[0]
    k = k_ref[0]
    v = v_ref[0]
    og = o_ref[0].astype(F32)
    ng = ng_ref[...]

    gts = gt_ref[0] + gb_ref[...]
    lane = lax.broadcasted_iota(jnp.int32, (L, LANES), 1)
    logsig = jnp.minimum(gts, 0.0) - jnp.log1p(jnp.exp(-jnp.abs(gts)))
    G = jnp.where(lane < H, gts, logsig)
    GT = G.T
    row = lax.broadcasted_iota(jnp.int32, (L, L), 0)
    col = lax.broadcasted_iota(jnp.int32, (L, L), 1)
    causal = col <= row
    tri = causal.astype(F32)
    triT = (row <= col).astype(F32)
    Bc = jnp.dot(tri, G, precision=HIGHEST, preferred_element_type=F32)
    Br = jnp.dot(GT[0:2 * H], triT, precision=HIGHEST, preferred_element_type=F32)

    for h in range(H):
        sl = slice(h * DH, (h + 1) * DH)
        qb, kb, vh = q[:, sl], k[:, sl], v[:, sl]
        qh = qb.astype(F32)
        kh = kb.astype(F32)
        bc = Bc[:, H + h:H + h + 1]
        br = Br[H + h:H + h + 1, :]
        igr = GT[h:h + 1, :]
        igc = G[:, h:h + 1]
        m0 = m_st[h:h + 1, 0:1]
        c0 = c_st[h]
        n0 = n_st[h:h + 1, :]

        logw = jnp.where(causal, bc - br + igr, -jnp.inf)
        m_inter = bc + m0
        m_t = jnp.maximum(m_inter, jnp.max(logw, axis=-1, keepdims=True))
        wts = jnp.exp(logw - m_t) * _dot_nt(qb, kb)
        s_inter = jnp.exp(m_inter - m_t)
        num = _dot(wts.astype(BF16), vh) + s_inter * _dot(qb, c0.astype(BF16))
        den = (jnp.sum(wts, axis=-1, keepdims=True)
               + s_inter * jnp.sum(qh * n0, axis=-1, keepdims=True))
        hh = num / jnp.maximum(jnp.abs(den), jnp.exp(-m_t))

        b_last = bc[L - 1:L, :]
        m_loc = jnp.max(b_last - br + igr, axis=-1, keepdims=True)
        m_new = jnp.maximum(b_last + m0, m_loc)
        kw = kh * jnp.exp(b_last - bc + igc - m_new)
        sp = jnp.exp(b_last + m0 - m_new)
        c_st[h] = sp * c0 + _dot(kw.T.astype(BF16), vh)
        n_st[h:h + 1, :] = sp * n0 + jnp.sum(kw, axis=0, keepdims=True)
        m_st[h:h + 1, :] = jnp.broadcast_to(m_new, (1, LANES))

        out_ref[0, :, sl] = (og[:, sl] * _rms(hh, ng[:, sl])).astype(out_ref.dtype)


def _mlstm(proj, gates, gate_bias, norm_g, *, L):
    B, T, _ = proj.shape
    W = MLSTM_HEADS * MLSTM_DH
    blk = lambda c0: pl.BlockSpec((1, L, W), lambda b, c: (b, c, c0))
    full = lambda a: pl.BlockSpec(a.shape, lambda b, c: (0,) * a.ndim)
    return pl.pallas_call(
        functools.partial(_mlstm_kernel, L=L),
        grid=(B, T // L),
        in_specs=[blk(0), blk(1), blk(2), blk(3),
                  pl.BlockSpec((1, L, LANES), lambda b, c: (b, c, 0)),
                  full(gate_bias), full(norm_g)],
        out_specs=pl.BlockSpec((1, L, W), lambda b, c: (b, c, 0)),
        out_shape=jax.ShapeDtypeStruct((B, T, W), BF16),
        scratch_shapes=[pltpu.VMEM((MLSTM_HEADS, MLSTM_DH, MLSTM_DH), F32),
                        pltpu.VMEM((8, MLSTM_DH), F32),
                        pltpu.VMEM((8, LANES), F32)],
        compiler_params=_params("parallel", "arbitrary"),
        name="mlstm",
    )(proj, proj, proj, proj, gates, gate_bias, norm_g)


def _diff_attn_kernel(slopes_ref, lam_ref, q_ref, k_ref, v_ref, ng_ref, out_ref,
                      vT, ta, tb, acc1, acc2, m1, m2, *, tq, lam_init):
    h = pl.program_id(1)
    qi = pl.program_id(2)
    slope = slopes_ref[h]
    half, dv = DIFF_DH, DIFF_DV

    vT[qi, 0:dv, :] = v_ref[0, pl.ds(pl.multiple_of(qi * tq, tq), tq), :].T
    ones_row = lax.broadcasted_iota(jnp.int32, (ACC_PAD, tq), 0) == 0
    vT[qi, dv:dv + ACC_PAD, :] = jnp.where(ones_row, 1.0, 0.0).astype(BF16)

    lane = lax.broadcasted_iota(jnp.int32, (tq, LANES), 1)
    r = lax.broadcasted_iota(jnp.int32, (tq, LANES), 0)
    r_lo = (r & 255).astype(F32)
    r_hi = (r - (r & 255)).astype(F32)
    sv = jnp.full((tq, LANES), slope, F32)
    s_hi = sv.astype(BF16).astype(F32)
    s_mid = (sv - s_hi).astype(BF16).astype(F32)
    s_lo = (sv - s_hi - s_mid).astype(BF16).astype(F32)
    qs = q_ref[0] * (half ** -0.5)

    def operands(f0):
        fl = lane - f0
        kfeat = jnp.where((fl >= 0) & (fl < 3), r_lo, jnp.where((fl >= 3) & (fl < 6), r_hi, 0.0))
        third = jnp.where(fl >= 3, fl - 3, fl)
        qfeat = jnp.where(third == 0, s_hi, jnp.where(third == 1, s_mid, s_lo))
        qfeat = jnp.where((fl >= 0) & (fl < 6), qfeat, 0.0)
        keep = (lane < half) if f0 >= half else (lane >= half)
        return kfeat.astype(BF16), keep, jnp.where(keep, qs, qfeat.astype(BF16))

    ops = (operands(half), operands(0))
    maps = ((acc1, m1), (acc2, m2))
    for acc, m in maps:
        acc[...] = jnp.zeros_like(acc)
        m[...] = jnp.full_like(m, NEG_BIG)

    krow = lax.broadcasted_iota(jnp.int32, (tq, tq), 0)
    qcol = lax.broadcasted_iota(jnp.int32, (tq, tq), 1)

    def scores(j, mi):
        kb = k_ref[0, pl.ds(pl.multiple_of(j * tq, tq), tq), :]
        kfeat, keep, qop = ops[mi]
        return _dot_nt(jnp.where(keep, kb, kfeat), qop)

    def step(j, cur, nxt, masked):
        vTb = vT[j]
        cst = slope * (jnp.zeros((1, tq), jnp.int32) + (j - qi) * tq).astype(F32)
        for mi, (acc, m) in enumerate(maps):
            if nxt is not None:
                nxt[mi] = scores(j + 1, mi)
            t = cur[mi]
            if masked:
                t = jnp.where(krow <= qcol, t, NEG_BIG)
            m_old = m[...]
            m_new = jnp.maximum(m_old, jnp.max(t, axis=0, keepdims=True) + cst)
            p = jnp.exp(t - (m_new - cst)).astype(BF16)
            acc[...] = jnp.exp(m_old - m_new) * acc[...] + _dot(vTb, p)
            m[...] = m_new

    for mi in range(2):
        ta[mi] = scores(0, mi)

    def pair(i, carry):
        step(2 * i, ta, tb, False)
        step(2 * i + 1, tb, ta, False)
        return carry

    def quad(i, carry):
        pair(2 * i, carry)
        pair(2 * i + 1, carry)
        return carry

    n_quads = lax.shift_right_logical(qi, 2)
    lax.fori_loop(0, n_quads, quad, 0)
    lax.fori_loop(2 * n_quads, lax.shift_right_logical(qi, 1), pair, 0)
    odd = (qi & 1) == 1

    @pl.when(odd)
    def _():
        step(qi - 1, ta, tb, False)
        step(qi, tb, None, True)

    @pl.when(jnp.logical_not(odd))
    def _():
        step(qi, ta, None, True)

    lp = lam_ref[...]
    lam = (jnp.exp(jnp.sum(lp[0:1] * lp[1:2], axis=-1, keepdims=True))
           - jnp.exp(jnp.sum(lp[2:3] * lp[3:4], axis=-1, keepdims=True)) + lam_init)
    a1 = acc1[...]
    a2 = acc2[...]
    o = (a1[:dv] / a1[dv:dv + 1] - lam * (a2[:dv] / a2[dv:dv + 1])).T
    out_ref[0] = (_rms(o, ng_ref[...]) * (1.0 - lam_init)).astype(out_ref.dtype)


def _diff_attn(proj, slopes, lam_params, norm_g, *, tq, lam_init, q0, k0, v0):
    B, T, _ = proj.shape
    H = DIFF_HEADS
    return pl.pallas_call(
        functools.partial(_diff_attn_kernel, tq=tq, lam_init=lam_init),
        grid=(B, H, T // tq),
        in_specs=[
            pl.BlockSpec(memory_space=pltpu.SMEM),
            pl.BlockSpec(lam_params.shape, lambda b, h, i: (0, 0)),
            pl.BlockSpec((1, tq, LANES), lambda b, h, i: (b, i, q0 + h)),
            pl.BlockSpec((1, T, LANES), lambda b, h, i: (b, 0, k0 + h)),
            pl.BlockSpec((1, T, LANES), lambda b, h, i: (b, 0, v0 + h)),
            pl.BlockSpec((1, LANES), lambda b, h, i: (0, h)),
        ],
        out_specs=pl.BlockSpec((1, tq, LANES), lambda b, h, i: (b, i, h)),
        out_shape=jax.ShapeDtypeStruct((B, T, H * DIFF_DV), BF16),
        scratch_shapes=[pltpu.VMEM((T // tq, DIFF_DV + ACC_PAD, tq), BF16),
                        pltpu.VMEM((2, tq, tq), F32), pltpu.VMEM((2, tq, tq), F32),
                        pltpu.VMEM((DIFF_DV + ACC_PAD, tq), F32),
                        pltpu.VMEM((DIFF_DV + ACC_PAD, tq), F32),
                        pltpu.VMEM((1, tq), F32), pltpu.VMEM((1, tq), F32)],
        compiler_params=_params("parallel", "parallel", "arbitrary"),
        name="diff_attn",
    )(slopes, lam_params, proj, proj, proj, norm_g)


def _ssd_kernel(z_ref, x_ref, b_ref, c_ref, dt_ref, hp_ref, ng_ref, out_ref,
                s_st, dt_s, acs_s, art_s, *, L, gps):
    pg = pl.program_id(2)

    @pl.when(pl.program_id(1) == 0)
    def _():
        for k in range(gps):
            s_st[pg * gps + k] = jnp.zeros(s_st.shape[1:], F32)

    row = lax.broadcasted_iota(jnp.int32, (L, L), 0)
    col = lax.broadcasted_iota(jnp.int32, (L, L), 1)
    causal = col <= row

    @pl.when(pg == 0)
    def _():
        hp = hp_ref[...]
        pre = dt_ref[0] + hp[0:1]
        dt = jnp.maximum(pre, 0.0) + jnp.log1p(jnp.exp(-jnp.abs(pre)))
        da = dt * (-jnp.exp(hp[1:2]))
        acs = jnp.dot(causal.astype(F32), da, precision=HIGHEST, preferred_element_type=F32)
        dt_s[...] = dt
        acs_s[...] = acs
        art_s[...] = acs.T

    for k in range(gps):
        gw = slice(k * SSM_GW, (k + 1) * SSM_GW)
        gn = slice(k * SSM_STATE, (k + 1) * SSM_STATE)
        y = _ssd_group(pg * gps + k, x_ref[0, :, gw].astype(F32), b_ref[0, :, gn], c_ref[0, :, gn],
                       hp_ref, s_st, dt_s, acs_s, art_s, causal, L)
        y = y * z_ref[0, :, gw].astype(F32)
        out_ref[0, :, gw] = _rms(y, ng_ref[:, gw]).astype(out_ref.dtype)


def _ssd_group(g, x, bmb, cmb, hp_ref, s_st, dt_s, acs_s, art_s, causal, L):
    R, P, GW = SSM_HPG, SSM_HEADDIM, SSM_GW

    shift = (LANES - R * g) & (LANES - 1)
    dtg = pltpu.roll(dt_s[...], shift, 1)
    acg = pltpu.roll(acs_s[...], shift, 1)
    dsk = pltpu.roll(hp_ref[...], shift, 1)[2:3]
    cols = jnp.concatenate([dtg, acg, jnp.broadcast_to(dsk, (HALO, LANES))], axis=0)
    hi = cols.astype(BF16)
    rest = cols - hi.astype(F32)
    mid = rest.astype(BF16)
    lo = (rest - mid.astype(F32)).astype(BF16)
    er = lax.broadcasted_iota(jnp.int32, (3 * LANES, GW), 0) & (LANES - 1)
    ec = lax.broadcasted_iota(jnp.int32, (3 * LANES, GW), 1)
    expand = ((ec >= er * P) & (ec < (er + 1) * P)).astype(BF16)
    spread = _dot(jnp.concatenate([hi, mid, lo], axis=1), expand)
    dt_f, ac_f, d_f = spread[:L], spread[L:2 * L], spread[2 * L:2 * L + 1]
    a_last = ac_f[L - 1:L, :]

    cb = _dot_nt(cmb, bmb)
    s0 = s_st[g]
    xdt = x * dt_f
    xdt_b = xdt.astype(BF16)
    y = jnp.exp(ac_f) * _dot(cmb, s0.astype(BF16)) + d_f * x
    lane = lax.broadcasted_iota(jnp.int32, (L, GW), 1)
    for r in range(R):
        ac = acg[:, r:r + 1]
        ar = art_s[pl.ds(g * R + r, 1), :]
        decay = jnp.exp(jnp.where(causal, ac - ar, -jnp.inf))
        own = (lane >= r * P) & (lane < (r + 1) * P)
        y = y + _dot((cb * decay).astype(BF16), jnp.where(own, xdt_b, jnp.zeros_like(xdt_b)))

    xw = (xdt * jnp.exp(a_last - ac_f)).astype(BF16)
    s_st[g] = jnp.exp(a_last) * s0 + _dot(bmb.T, xw)
    return y


def _ssd(proj, dtraw, head_params, norm_g, *, L, gps):
    B, T, _ = proj.shape
    G = SSM_GROUPS
    GW, N = gps * SSM_GW, gps * SSM_STATE
    inner = SSM_GROUPS * SSM_GW
    xo, bo, co = inner // GW, 2 * inner // N, (2 * inner + G * SSM_STATE) // N
    return pl.pallas_call(
        functools.partial(_ssd_kernel, L=L, gps=gps),
        grid=(B, T // L, G // gps),
        in_specs=[
            pl.BlockSpec((1, L, GW), lambda b, c, g: (b, c, g)),
            pl.BlockSpec((1, L, GW), lambda b, c, g: (b, c, xo + g)),
            pl.BlockSpec((1, L, N), lambda b, c, g: (b, c, bo + g)),
            pl.BlockSpec((1, L, N), lambda b, c, g: (b, c, co + g)),
            pl.BlockSpec((1, L, LANES), lambda b, c, g: (b, c, 0)),
            pl.BlockSpec(head_params.shape, lambda b, c, g: (0, 0)),
            pl.BlockSpec((1, GW), lambda b, c, g: (0, g)),
        ],
        out_specs=pl.BlockSpec((1, L, GW), lambda b, c, g: (b, c, g)),
        out_shape=jax.ShapeDtypeStruct((B, T, inner), BF16),
        scratch_shapes=[pltpu.VMEM((G, SSM_STATE, SSM_GW), F32),
                        pltpu.VMEM((L, LANES), F32),
                        pltpu.VMEM((L, LANES), F32),
                        pltpu.VMEM((LANES, L), F32)],
        compiler_params=_params("parallel", "arbitrary", "arbitrary"),
        name="ssd",
    )(proj, proj, proj, proj, dtraw, head_params, norm_g)


def _proj_residual_kernel(*refs, n_in):
    x_ref = refs[0]
    a_refs = refs[1:1 + n_in]
    w_refs = refs[1 + n_in:1 + 2 * n_in]
    o_ref = refs[1 + 2 * n_in]
    acc = x_ref[...]
    for a_ref, w_ref in zip(a_refs, w_refs):
        acc = acc + _dot(a_ref[...], w_ref[...])
    o_ref[...] = acc


def _proj_residual(x, acts, weights, *, tm):
    M, D = x.shape
    n_in = len(acts)
    return pl.pallas_call(
        functools.partial(_proj_residual_kernel, n_in=n_in),
        grid=(M // tm,),
        in_specs=([pl.BlockSpec((tm, D), lambda i: (i, 0))]
                  + [pl.BlockSpec((tm, a.shape[1]), lambda i: (i, 0)) for a in acts]
                  + [pl.BlockSpec(w.shape, lambda i: (0, 0)) for w in weights]),
        out_specs=pl.BlockSpec((tm, D), lambda i: (i, 0)),
        out_shape=jax.ShapeDtypeStruct((M, D), F32),
        compiler_params=_params("parallel"),
        name="outproj_residual",
    )(x, *acts, *weights)


def _mlp_kernel(x_ref, g_ref, w1_ref, w2_ref, fg_ref, o_ref, *, tf, final_norm):
    x = x_ref[...]
    hb = _rms(x, g_ref[...]).astype(BF16)
    acc = x
    for c in range(w1_ref.shape[1] // tf):
        u = jnp.maximum(_dot(hb, w1_ref[:, c * tf:(c + 1) * tf]), 0.0)
        acc = acc + _dot((u * u).astype(BF16), w2_ref[c * tf:(c + 1) * tf, :])
    if final_norm:
        acc = _rms(acc, fg_ref[...])
    o_ref[...] = acc


def _mlp(x, g, w1, w2, final_g, *, tm, tf, final_norm):
    M, D = x.shape
    const = lambda a: pl.BlockSpec(a.shape, lambda i: (0, 0), pipeline_mode=pl.Buffered(1))
    return pl.pallas_call(
        functools.partial(_mlp_kernel, tf=tf, final_norm=final_norm),
        grid=(M // tm,),
        in_specs=[pl.BlockSpec((tm, D), lambda i: (i, 0)), const(g), const(w1), const(w2),
                  const(final_g)],
        out_specs=pl.BlockSpec((tm, D), lambda i: (i, 0)),
        out_shape=jax.ShapeDtypeStruct((M, D), F32),
        compiler_params=_params("parallel"),
        name="mlp",
    )(x, g, w1, w2, final_g)


def _pad_cols(a, n):
    return jnp.pad(a, ((0, 0), (0, n - a.shape[1])))


def _tile(n, pref):
    return min(n, pref)


def kernel(x, norm_mix_g, norm_mlp_g, ab_w_in, ab_conv_w, ab_conv_b, ab_i_bias, ab_f_bias, ab_mnorm_g, ab_lam_q1, ab_lam_k1, ab_lam_q2, ab_lam_k2, ab_dnorm_g, ab_w_out, ssm_w_in, ssm_conv_w, ssm_conv_b, ssm_dt_bias, ssm_a_log, ssm_d, ssm_norm_g, ssm_w_out, mlp_w1, mlp_w2, final_g):
    B, T, D = x.shape
    M = B * T
    depth = norm_mix_g.shape[0]
    tm_proj = _tile(M, 1024)
    tm_mlp = _tile(T, 512)
    tn = 512
    tm_in = _tile(T, 512)
    row = lambda a: a.reshape(1, -1).astype(F32)
    xf = x.reshape(M, D)

    for i in range(depth):
        j = i // 2
        if i % 2 == 0:
            w = ab_w_in[j]
            mw = MLSTM_HEADS * MLSTM_DH
            g0 = 4 * mw
            w_main = jnp.concatenate([w[:, :g0], w[:, g0 + 2 * MLSTM_HEADS:]], axis=1).astype(BF16)
            w_gate = _pad_cols(w[:, g0:g0 + 2 * MLSTM_HEADS], LANES).astype(BF16)
            acts = ((("conv", 1.0),) * (mw // tn) + (("conv", MLSTM_DH ** -0.5),) * (mw // tn)
                    + (("none", 1.0),) * (mw // tn) + (("sigmoid", 1.0),) * (mw // tn)
                    + (("none", 1.0),) * ((w_main.shape[1] - g0) // tn))
            proj, gates = _norm_matmul(xf, row(norm_mix_g[i]), w_main, w_gate, ab_conv_w[j],
                                       row(ab_conv_b[j]), tm=tm_in, tn=tn, acts=acts, conv0=0,
                                       tiles_per_seq=T // tm_in)
            proj = proj.reshape(B, T, -1)
            gates = gates.reshape(B, T, LANES)
            gate_bias = _pad_cols(jnp.concatenate([ab_i_bias[j], ab_f_bias[j]]).reshape(1, -1), LANES)
            hm = _mlstm(proj, gates, gate_bias, row(ab_mnorm_g[j]), L=_tile(T, 256))
            lam_init = 0.8 - 0.6 * math.exp(-0.3 * i)
            slopes = jnp.exp2(-8.0 * jnp.arange(1, DIFF_HEADS + 1, dtype=F32) / DIFF_HEADS)
            lam_params = jnp.stack([ab_lam_q1[j], ab_lam_k1[j], ab_lam_q2[j], ab_lam_k2[j]])
            qblk = g0 // LANES
            hd = _diff_attn(proj, slopes, lam_params, row(ab_dnorm_g[j]), tq=_tile(T, 512),
                            lam_init=lam_init, q0=qblk, k0=qblk + DIFF_HEADS, v0=qblk + 2 * DIFF_HEADS)
            wo = ab_w_out[j].astype(BF16)
            half = MLSTM_HEADS * MLSTM_DH
            xf = _proj_residual(xf, [hm.reshape(M, -1), hd.reshape(M, -1)], [wo[:half], wo[half:]],
                                tm=tm_proj)
        else:
            w = ssm_w_in[j]
            n_main = w.shape[1] - SSM_GROUPS * SSM_HPG
            w_main = w[:, :n_main].astype(BF16)
            w_dt = _pad_cols(w[:, n_main:], LANES).astype(BF16)
            inner = SSM_GROUPS * SSM_GW
            acts = (("silu", 1.0),) * (inner // tn) + (("conv", 1.0),) * ((n_main - inner) // tn)
            proj, dtraw = _norm_matmul(xf, row(norm_mix_g[i]), w_main, w_dt, ssm_conv_w[j],
                                       row(ssm_conv_b[j]), tm=tm_in, tn=tn, acts=acts, conv0=inner,
                                       tiles_per_seq=T // tm_in)
            head_params = jnp.pad(jnp.stack([ssm_dt_bias[j], ssm_a_log[j], ssm_d[j]]),
                                  ((0, HALO - 3), (0, LANES - SSM_GROUPS * SSM_HPG)))
            y = _ssd(proj.reshape(B, T, -1), dtraw.reshape(B, T, LANES), head_params,
                     row(ssm_norm_g[j]), L=_tile(T, 256), gps=8)
            xf = _proj_residual(xf, [y.reshape(M, -1)], [ssm_w_out[j].astype(BF16)], tm=tm_proj)
        xf = _mlp(xf, row(norm_mlp_g[i]), mlp_w1[i].astype(BF16), mlp_w2[i].astype(BF16),
                  row(final_g), tm=tm_mlp, tf=512, final_norm=(i == depth - 1))
    return xf.reshape(B, T, D)
```

```python
import functools
import math

import jax
import jax.numpy as jnp
from jax import lax
from jax.experimental import pallas as pl
from jax.experimental.pallas import tpu as pltpu

F32 = jnp.float32
BF16 = jnp.bfloat16
HIGHEST = lax.Precision.HIGHEST

RMS_EPS = 1e-5
LOG2E = 1.4426950408889634
CONV_WIDTH = 4
LANES = 128
HALO = 8
NEG_BIG = -1e30
ACC_PAD = 16
VMEM_LIMIT = 56 * 1024 * 1024

MLSTM_HEADS = 4
MLSTM_DH = 128
DIFF_HEADS = 4
DIFF_DH = 64
DIFF_DV = 128
SSM_GROUPS = 8
SSM_HPG = 4
SSM_HEADDIM = 64
SSM_STATE = 128
SSM_GW = SSM_HPG * SSM_HEADDIM


def _params(*sem, flags=None):
    return pltpu.CompilerParams(dimension_semantics=sem, vmem_limit_bytes=VMEM_LIMIT, flags=flags)


def _sigmoid(x):
    return 1.0 / (1.0 + jnp.exp2(x * -LOG2E))


def _rms(x, g):
    return x * lax.rsqrt(jnp.mean(x * x, axis=-1, keepdims=True) + RMS_EPS) * g


def _dot(a, b):
    return jnp.dot(a, b, preferred_element_type=F32)


def _dot_nt(a, b):
    return lax.dot_general(a, b, (((1,), (1,)), ((), ())), preferred_element_type=F32)


def _conv_silu(buf, x, prev, w, b):
    L = x.shape[0]
    buf[0:HALO, :] = prev
    buf[HALO:HALO + L, :] = x
    y = x * w[CONV_WIDTH - 1:CONV_WIDTH] + b
    for s in range(1, CONV_WIDTH):
        y = y + buf[HALO - s:HALO - s + L, :] * w[CONV_WIDTH - 1 - s:CONV_WIDTH - s]
    return y * _sigmoid(y)


def _norm_matmul_kernel(x_ref, g_ref, w_ref, wg_ref, cw_ref, cb_ref, o_ref, og_ref,
                        halo, buf0, buf1, *, tn, acts, order, conv0, tiles_per_seq):
    tm = x_ref.shape[0]

    @pl.when(pl.program_id(0) % tiles_per_seq == 0)
    def _():
        halo[...] = jnp.zeros_like(halo)

    hb = _rms(x_ref[...], g_ref[...]).astype(BF16)
    og_ref[...] = _dot(hb, wg_ref[...])
    bufs = [buf0, buf1]

    def finish(c, acc):
        kind, scale = acts[c]
        cols = slice(c * tn, (c + 1) * tn)
        if kind == "conv":
            cc = slice(c * tn - conv0, (c + 1) * tn - conv0)
            prev = halo[:, cc]
            halo[:, cc] = acc[tm - HALO:]
            acc = _conv_silu(bufs[0], acc, prev, cw_ref[:, cc], cb_ref[:, cc])
            bufs.reverse()
        elif kind == "silu":
            acc = acc * _sigmoid(acc)
        elif kind == "sigmoid":
            acc = _sigmoid(acc)
        if scale != 1.0:
            acc = acc * scale
        o_ref[:, cols] = acc.astype(o_ref.dtype)

    pending = None
    for c in order:
        acc = _dot(hb, w_ref[:, c * tn:(c + 1) * tn])
        if pending is not None:
            finish(*pending)
        pending = (c, acc)
    finish(*pending)


def _norm_matmul(x, g, w, wg, conv_w, conv_b, *, tm, tn, acts, conv0, tiles_per_seq):
    M, D = x.shape
    N = w.shape[1]
    order = range(len(acts))
    const = lambda a: pl.BlockSpec(a.shape, lambda i: (0, 0), pipeline_mode=pl.Buffered(1))
    return pl.pallas_call(
        functools.partial(_norm_matmul_kernel, tn=tn, acts=acts, order=tuple(order), conv0=conv0,
                          tiles_per_seq=tiles_per_seq),
        grid=(M // tm,),
        in_specs=[pl.BlockSpec((tm, D), lambda i: (i, 0)), const(g), const(w), const(wg),
                  const(conv_w), const(conv_b)],
        out_specs=[
            pl.BlockSpec((tm, N), lambda i: (i, 0)),
            pl.BlockSpec((tm, LANES), lambda i: (i, 0)),
        ],
        out_shape=[jax.ShapeDtypeStruct((M, N), BF16), jax.ShapeDtypeStruct((M, LANES), F32)],
        scratch_shapes=[pltpu.VMEM((HALO, conv_w.shape[1]), F32),
                        pltpu.VMEM((HALO + tm, tn), F32),
                        pltpu.VMEM((HALO + tm, tn), F32)],
        compiler_params=_params("arbitrary"),
        name="norm_inproj",
    )(x, g, w, wg, conv_w, conv_b)


def _mlstm_kernel(q_ref, k_ref, v_ref, o_ref, gt_ref, gb_ref, ng_ref, out_ref,
                  c_st, n_st, m_st, *, L):
    H, DH = MLSTM_HEADS, MLSTM_DH

    @pl.when(pl.program_id(1) == 0)
    def _():
        c_st[...] = jnp.zeros_like(c_st)
        n_st[...] = jnp.zeros_like(n_st)
        m_st[...] = jnp.zeros_like(m_st)

    q = q_ref[0]
    k = k_ref[0]
    v = v_ref[0]
    og = o_ref[0].astype(F32)
    ng = ng_ref[...]

    gts = gt_ref[0] + gb_ref[...]
    lane = lax.broadcasted_iota(jnp.int32, (L, LANES), 1)
    logsig = jnp.minimum(gts, 0.0) - jnp.log1p(jnp.exp(-jnp.abs(gts)))
    G = jnp.where(lane < H, gts, logsig)
    GT = G.T
    row = lax.broadcasted_iota(jnp.int32, (L, L), 0)
    col = lax.broadcasted_iota(jnp.int32, (L, L), 1)
    causal = col <= row
    tri = causal.astype(F32)
    triT = (row <= col).astype(F32)
    Bc = jnp.dot(tri, G, precision=HIGHEST, preferred_element_type=F32)
    Br = jnp.dot(GT[0:2 * H], triT, precision=HIGHEST, preferred_element_type=F32)

    for h in range(H):
        sl = slice(h * DH, (h + 1) * DH)
        qb, kb, vh = q[:, sl], k[:, sl], v[:, sl]
        qh = qb.astype(F32)
        kh = kb.astype(F32)
        bc = Bc[:, H + h:H + h + 1]
        br = Br[H + h:H + h + 1, :]
        igr = GT[h:h + 1, :]
        igc = G[:, h:h + 1]
        m0 = m_st[h:h + 1, 0:1]
        c0 = c_st[h]
        n0 = n_st[h:h + 1, :]

        logw = jnp.where(causal, bc - br + igr, -jnp.inf)
        m_inter = bc + m0
        m_t = jnp.maximum(m_inter, jnp.max(logw, axis=-1, keepdims=True))
        wts = jnp.exp(logw - m_t) * _dot_nt(qb, kb)
        s_inter = jnp.exp(m_inter - m_t)
        num = _dot(wts.astype(BF16), vh) + s_inter * _dot(qb, c0.astype(BF16))
        den = (jnp.sum(wts, axis=-1, keepdims=True)
               + s_inter * jnp.sum(qh * n0, axis=-1, keepdims=True))
        hh = num / jnp.maximum(jnp.abs(den), jnp.exp(-m_t))

        b_last = bc[L - 1:L, :]
        m_loc = jnp.max(b_last - br + igr, axis=-1, keepdims=True)
        m_new = jnp.maximum(b_last + m0, m_loc)
        kw = kh * jnp.exp(b_last - bc + igc - m_new)
        sp = jnp.exp(b_last + m0 - m_new)
        c_st[h] = sp * c0 + _dot(kw.T.astype(BF16), vh)
        n_st[h:h + 1, :] = sp * n0 + jnp.sum(kw, axis=0, keepdims=True)
        m_st[h:h + 1, :] = jnp.broadcast_to(m_new, (1, LANES))

        out_ref[0, :, sl] = (og[:, sl] * _rms(hh, ng[:, sl])).astype(out_ref.dtype)


def _mlstm(proj, gates, gate_bias, norm_g, *, L):
    B, T, _ = proj.shape
    W = MLSTM_HEADS * MLSTM_DH
    blk = lambda c0: pl.BlockSpec((1, L, W), lambda b, c: (b, c, c0))
    full = lambda a: pl.BlockSpec(a.shape, lambda b, c: (0,) * a.ndim)
    return pl.pallas_call(
        functools.partial(_mlstm_kernel, L=L),
        grid=(B, T // L),
        in_specs=[blk(0), blk(1), blk(2), blk(3),
                  pl.BlockSpec((1, L, LANES), lambda b, c: (b, c, 0)),
                  full(gate_bias), full(norm_g)],
        out_specs=pl.BlockSpec((1, L, W), lambda b, c: (b, c, 0)),
        out_shape=jax.ShapeDtypeStruct((B, T, W), BF16),
        scratch_shapes=[pltpu.VMEM((MLSTM_HEADS, MLSTM_DH, MLSTM_DH), F32),
                        pltpu.VMEM((8, MLSTM_DH), F32),
                        pltpu.VMEM((8, LANES), F32)],
        compiler_params=_params("parallel", "arbitrary"),
        name="mlstm",
    )(proj, proj, proj, proj, gates, gate_bias, norm_g)


def _diff_attn_kernel(slopes_ref, lam_ref, q_ref, k_ref, v_ref, ng_ref, out_ref,
                      vT, ta, tb, acc1, acc2, m1, m2, *, tq, lam_init):
    h = pl.program_id(1)
    qi = pl.program_id(2)
    slope = slopes_ref[h]
    half, dv = DIFF_DH, DIFF_DV

    vT[qi, 0:dv, :] = v_ref[0, pl.ds(pl.multiple_of(qi * tq, tq), tq), :].T
    ones_row = lax.broadcasted_iota(jnp.int32, (ACC_PAD, tq), 0) == 0
    vT[qi, dv:dv + ACC_PAD, :] = jnp.where(ones_row, 1.0, 0.0).astype(BF16)

    lane = lax.broadcasted_iota(jnp.int32, (tq, LANES), 1)
    r = lax.broadcasted_iota(jnp.int32, (tq, LANES), 0)
    r_lo = (r & 255).astype(F32)
    r_hi = (r - (r & 255)).astype(F32)
    sv = jnp.full((tq, LANES), slope, F32)
    s_hi = sv.astype(BF16).astype(F32)
    s_mid = (sv - s_hi).astype(BF16).astype(F32)
    s_lo = (sv - s_hi - s_mid).astype(BF16).astype(F32)
    qs = q_ref[0] * (half ** -0.5)

    def operands(f0):
        fl = lane - f0
        kfeat = jnp.where((fl >= 0) & (fl < 3), r_lo, jnp.where((fl >= 3) & (fl < 6), r_hi, 0.0))
        third = jnp.where(fl >= 3, fl - 3, fl)
        qfeat = jnp.where(third == 0, s_hi, jnp.where(third == 1, s_mid, s_lo))
        qfeat = jnp.where((fl >= 0) & (fl < 6), qfeat, 0.0)
        keep = (lane < half) if f0 >= half else (lane >= half)
        return kfeat.astype(BF16), keep, jnp.where(keep, qs, qfeat.astype(BF16))

    ops = (operands(half), operands(0))
    maps = ((acc1, m1), (acc2, m2))
    for acc, m in maps:
        acc[...] = jnp.zeros_like(acc)
        m[...] = jnp.full_like(m, NEG_BIG)

    krow = lax.broadcasted_iota(jnp.int32, (tq, tq), 0)
    qcol = lax.broadcasted_iota(jnp.int32, (tq, tq), 1)

    def scores(j, mi):
        kb = k_ref[0, pl.ds(pl.multiple_of(j * tq, tq), tq), :]
        kfeat, keep, qop = ops[mi]
        return _dot_nt(jnp.where(keep, kb, kfeat), qop)

    def step(j, cur, nxt, masked):
        vTb = vT[j]
        cst = slope * (jnp.zeros((1, tq), jnp.int32) + (j - qi) * tq).astype(F32)
        for mi, (acc, m) in enumerate(maps):
            if nxt is not None:
                nxt[mi] = scores(j + 1, mi)
            t = cur[mi]
            if masked:
                t = jnp.where(krow <= qcol, t, NEG_BIG)
            m_old = m[...]
            m_new = jnp.maximum(m_old, jnp.max(t, axis=0, keepdims=True) + cst)
            p = jnp.exp(t - (m_new - cst)).astype(BF16)
            acc[...] = jnp.exp(m_old - m_new) * acc[...] + _dot(vTb, p)
            m[...] = m_new

    for mi in range(2):
        ta[mi] = scores(0, mi)

    def pair(i, carry):
        step(2 * i, ta, tb, False)
        step(2 * i + 1, tb, ta, False)
        return carry

    def quad(i, carry):
        pair(2 * i, carry)
        pair(2 * i + 1, carry)
        return carry

    def octet(i, carry):
        quad(2 * i, carry)
        quad(2 * i + 1, carry)
        return carry

    n_octets = lax.shift_right_logical(qi, 3)
    n_quads = lax.shift_right_logical(qi, 2)
    lax.fori_loop(0, n_octets, octet, 0)
    lax.fori_loop(2 * n_octets, n_quads, quad, 0)
    lax.fori_loop(2 * n_quads, lax.shift_right_logical(qi, 1), pair, 0)
    odd = (qi & 1) == 1

    @pl.when(odd)
    def _():
        step(qi - 1, ta, tb, False)
        step(qi, tb, None, True)

    @pl.when(jnp.logical_not(odd))
    def _():
        step(qi, ta, None, True)

    lp = lam_ref[...]
    lam = (jnp.exp(jnp.sum(lp[0:1] * lp[1:2], axis=-1, keepdims=True))
           - jnp.exp(jnp.sum(lp[2:3] * lp[3:4], axis=-1, keepdims=True)) + lam_init)
    a1 = acc1[...]
    a2 = acc2[...]
    o = (a1[:dv] / a1[dv:dv + 1] - lam * (a2[:dv] / a2[dv:dv + 1])).T
    out_ref[0] = (_rms(o, ng_ref[...]) * (1.0 - lam_init)).astype(out_ref.dtype)


def _diff_attn(proj, slopes, lam_params, norm_g, *, tq, lam_init, q0, k0, v0):
    B, T, _ = proj.shape
    H = DIFF_HEADS
    return pl.pallas_call(
        functools.partial(_diff_attn_kernel, tq=tq, lam_init=lam_init),
        grid=(B, H, T // tq),
        in_specs=[
            pl.BlockSpec(memory_space=pltpu.SMEM),
            pl.BlockSpec(lam_params.shape, lambda b, h, i: (0, 0)),
            pl.BlockSpec((1, tq, LANES), lambda b, h, i: (b, i, q0 + h)),
            pl.BlockSpec((1, T, LANES), lambda b, h, i: (b, 0, k0 + h)),
            pl.BlockSpec((1, T, LANES), lambda b, h, i: (b, 0, v0 + h)),
            pl.BlockSpec((1, LANES), lambda b, h, i: (0, h)),
        ],
        out_specs=pl.BlockSpec((1, tq, LANES), lambda b, h, i: (b, i, h)),
        out_shape=jax.ShapeDtypeStruct((B, T, H * DIFF_DV), BF16),
        scratch_shapes=[pltpu.VMEM((T // tq, DIFF_DV + ACC_PAD, tq), BF16),
                        pltpu.VMEM((2, tq, tq), F32), pltpu.VMEM((2, tq, tq), F32),
                        pltpu.VMEM((DIFF_DV + ACC_PAD, tq), F32),
                        pltpu.VMEM((DIFF_DV + ACC_PAD, tq), F32),
                        pltpu.VMEM((1, tq), F32), pltpu.VMEM((1, tq), F32)],
        compiler_params=_params("parallel", "parallel", "arbitrary"),
        name="diff_attn",
    )(slopes, lam_params, proj, proj, proj, norm_g)


def _ssd_kernel(z_ref, x_ref, b_ref, c_ref, dt_ref, hp_ref, ng_ref, out_ref,
                s_st, dt_s, acs_s, art_s, *, L, gps):
    pg = pl.program_id(2)

    @pl.when(pl.program_id(1) == 0)
    def _():
        for k in range(gps):
            s_st[pg * gps + k] = jnp.zeros(s_st.shape[1:], F32)

    row = lax.broadcasted_iota(jnp.int32, (L, L), 0)
    col = lax.broadcasted_iota(jnp.int32, (L, L), 1)
    causal = col <= row

    @pl.when(pg == 0)
    def _():
        hp = hp_ref[...]
        pre = dt_ref[0] + hp[0:1]
        dt = jnp.maximum(pre, 0.0) + jnp.log1p(jnp.exp(-jnp.abs(pre)))
        da = dt * (-jnp.exp(hp[1:2]))
        acs = jnp.dot(causal.astype(F32), da, precision=HIGHEST, preferred_element_type=F32)
        dt_s[...] = dt
        acs_s[...] = acs
        art_s[...] = acs.T

    for k in range(gps):
        gw = slice(k * SSM_GW, (k + 1) * SSM_GW)
        gn = slice(k * SSM_STATE, (k + 1) * SSM_STATE)
        y = _ssd_group(pg * gps + k, x_ref[0, :, gw].astype(F32), b_ref[0, :, gn], c_ref[0, :, gn],
                       hp_ref, s_st, dt_s, acs_s, art_s, causal, L)
        y = y * z_ref[0, :, gw].astype(F32)
        out_ref[0, :, gw] = _rms(y, ng_ref[:, gw]).astype(out_ref.dtype)


def _ssd_group(g, x, bmb, cmb, hp_ref, s_st, dt_s, acs_s, art_s, causal, L):
    R, P, GW = SSM_HPG, SSM_HEADDIM, SSM_GW

    shift = (LANES - R * g) & (LANES - 1)
    dtg = pltpu.roll(dt_s[...], shift, 1)
    acg = pltpu.roll(acs_s[...], shift, 1)
    dsk = pltpu.roll(hp_ref[...], shift, 1)[2:3]
    cols = jnp.concatenate([dtg, acg, jnp.broadcast_to(dsk, (HALO, LANES))], axis=0)
    hi = cols.astype(BF16)
    rest = cols - hi.astype(F32)
    mid = rest.astype(BF16)
    lo = (rest - mid.astype(F32)).astype(BF16)
    er = lax.broadcasted_iota(jnp.int32, (3 * LANES, GW), 0) & (LANES - 1)
    ec = lax.broadcasted_iota(jnp.int32, (3 * LANES, GW), 1)
    expand = ((ec >= er * P) & (ec < (er + 1) * P)).astype(BF16)
    spread = _dot(jnp.concatenate([hi, mid, lo], axis=1), expand)
    dt_f, ac_f, d_f = spread[:L], spread[L:2 * L], spread[2 * L:2 * L + 1]
    a_last = ac_f[L - 1:L, :]

    cb = _dot_nt(cmb, bmb)
    s0 = s_st[g]
    xdt = x * dt_f
    xdt_b = xdt.astype(BF16)
    y = jnp.exp(ac_f) * _dot(cmb, s0.astype(BF16)) + d_f * x
    lane = lax.broadcasted_iota(jnp.int32, (L, GW), 1)
    for r in range(R):
        ac = acg[:, r:r + 1]
        ar = art_s[pl.ds(g * R + r, 1), :]
        decay = jnp.exp(jnp.where(causal, ac - ar, -jnp.inf))
        own = (lane >= r * P) & (lane < (r + 1) * P)
        y = y + _dot((cb * decay).astype(BF16), jnp.where(own, xdt_b, jnp.zeros_like(xdt_b)))

    xw = (xdt * jnp.exp(a_last - ac_f)).astype(BF16)
    s_st[g] = jnp.exp(a_last) * s0 + _dot(bmb.T, xw)
    return y


def _ssd(proj, dtraw, head_params, norm_g, *, L, gps):
    B, T, _ = proj.shape
    G = SSM_GROUPS
    GW, N = gps * SSM_GW, gps * SSM_STATE
    inner = SSM_GROUPS * SSM_GW
    xo, bo, co = inner // GW, 2 * inner // N, (2 * inner + G * SSM_STATE) // N
    return pl.pallas_call(
        functools.partial(_ssd_kernel, L=L, gps=gps),
        grid=(B, T // L, G // gps),
        in_specs=[
            pl.BlockSpec((1, L, GW), lambda b, c, g: (b, c, g)),
            pl.BlockSpec((1, L, GW), lambda b, c, g: (b, c, xo + g)),
            pl.BlockSpec((1, L, N), lambda b, c, g: (b, c, bo + g)),
            pl.BlockSpec((1, L, N), lambda b, c, g: (b, c, co + g)),
            pl.BlockSpec((1, L, LANES), lambda b, c, g: (b, c, 0)),
            pl.BlockSpec(head_params.shape, lambda b, c, g: (0, 0)),
            pl.BlockSpec((1, GW), lambda b, c, g: (0, g)),
        ],
        out_specs=pl.BlockSpec((1, L, GW), lambda b, c, g: (b, c, g)),
        out_shape=jax.ShapeDtypeStruct((B, T, inner), BF16),
        scratch_shapes=[pltpu.VMEM((G, SSM_STATE, SSM_GW), F32),
                        pltpu.VMEM((L, LANES), F32),
                        pltpu.VMEM((L, LANES), F32),
                        pltpu.VMEM((LANES, L), F32)],
        compiler_params=_params("parallel", "arbitrary", "arbitrary"),
        name="ssd",
    )(proj, proj, proj, proj, dtraw, head_params, norm_g)


def _proj_residual_kernel(*refs, n_in):
    x_ref = refs[0]
    a_refs = refs[1:1 + n_in]
    w_refs = refs[1 + n_in:1 + 2 * n_in]
    o_ref = refs[1 + 2 * n_in]
    acc = x_ref[...]
    for a_ref, w_ref in zip(a_refs, w_refs):
        acc = acc + _dot(a_ref[...], w_ref[...])
    o_ref[...] = acc


def _proj_residual(x, acts, weights, *, tm):
    M, D = x.shape
    n_in = len(acts)
    return pl.pallas_call(
        functools.partial(_proj_residual_kernel, n_in=n_in),
        grid=(M // tm,),
        in_specs=([pl.BlockSpec((tm, D), lambda i: (i, 0))]
                  + [pl.BlockSpec((tm, a.shape[1]), lambda i: (i, 0)) for a in acts]
                  + [pl.BlockSpec(w.shape, lambda i: (0, 0)) for w in weights]),
        out_specs=pl.BlockSpec((tm, D), lambda i: (i, 0)),
        out_shape=jax.ShapeDtypeStruct((M, D), F32),
        compiler_params=_params("parallel"),
        name="outproj_residual",
    )(x, *acts, *weights)


def _mlp_kernel(x_ref, g_ref, w1_ref, w2_ref, fg_ref, o_ref, *, tf, final_norm):
    x = x_ref[...]
    hb = _rms(x, g_ref[...]).astype(BF16)
    acc = x
    for c in range(w1_ref.shape[1] // tf):
        u = jnp.maximum(_dot(hb, w1_ref[:, c * tf:(c + 1) * tf]), 0.0)
        acc = acc + _dot((u * u).astype(BF16), w2_ref[c * tf:(c + 1) * tf, :])
    if final_norm:
        acc = _rms(acc, fg_ref[...])
    o_ref[...] = acc


def _mlp(x, g, w1, w2, final_g, *, tm, tf, final_norm):
    M, D = x.shape
    const = lambda a: pl.BlockSpec(a.shape, lambda i: (0, 0), pipeline_mode=pl.Buffered(1))
    return pl.pallas_call(
        functools.partial(_mlp_kernel, tf=tf, final_norm=final_norm),
        grid=(M // tm,),
        in_specs=[pl.BlockSpec((tm, D), lambda i: (i, 0)), const(g), const(w1), const(w2),
                  const(final_g)],
        out_specs=pl.BlockSpec((tm, D), lambda i: (i, 0)),
        out_shape=jax.ShapeDtypeStruct((M, D), F32),
        compiler_params=_params("parallel"),
        name="mlp",
    )(x, g, w1, w2, final_g)


def _pad_cols(a, n):
    return jnp.pad(a, ((0, 0), (0, n - a.shape[1])))


def _tile(n, pref):
    return min(n, pref)


def kernel(x, norm_mix_g, norm_mlp_g, ab_w_in, ab_conv_w, ab_conv_b, ab_i_bias, ab_f_bias, ab_mnorm_g, ab_lam_q1, ab_lam_k1, ab_lam_q2, ab_lam_k2, ab_dnorm_g, ab_w_out, ssm_w_in, ssm_conv_w, ssm_conv_b, ssm_dt_bias, ssm_a_log, ssm_d, ssm_norm_g, ssm_w_out, mlp_w1, mlp_w2, final_g):
    B, T, D = x.shape
    M = B * T
    depth = norm_mix_g.shape[0]
    tm_proj = _tile(M, 1024)
    tm_mlp = _tile(T, 512)
    tn = 512
    tm_in = _tile(T, 512)
    row = lambda a: a.reshape(1, -1).astype(F32)
    xf = x.reshape(M, D)

    for i in range(depth):
        j = i // 2
        if i % 2 == 0:
            w = ab_w_in[j]
            mw = MLSTM_HEADS * MLSTM_DH
            g0 = 4 * mw
            w_main = jnp.concatenate([w[:, :g0], w[:, g0 + 2 * MLSTM_HEADS:]], axis=1).astype(BF16)
            w_gate = _pad_cols(w[:, g0:g0 + 2 * MLSTM_HEADS], LANES).astype(BF16)
            acts = ((("conv", 1.0),) * (mw // tn) + (("conv", MLSTM_DH ** -0.5),) * (mw // tn)
                    + (("none", 1.0),) * (mw // tn) + (("sigmoid", 1.0),) * (mw // tn)
                    + (("none", 1.0),) * ((w_main.shape[1] - g0) // tn))
            proj, gates = _norm_matmul(xf, row(norm_mix_g[i]), w_main, w_gate, ab_conv_w[j],
                                       row(ab_conv_b[j]), tm=tm_in, tn=tn, acts=acts, conv0=0,
                                       tiles_per_seq=T // tm_in)
            proj = proj.reshape(B, T, -1)
            gates = gates.reshape(B, T, LANES)
            gate_bias = _pad_cols(jnp.concatenate([ab_i_bias[j], ab_f_bias[j]]).reshape(1, -1), LANES)
            hm = _mlstm(proj, gates, gate_bias, row(ab_mnorm_g[j]), L=_tile(T, 256))
            lam_init = 0.8 - 0.6 * math.exp(-0.3 * i)
            slopes = jnp.exp2(-8.0 * jnp.arange(1, DIFF_HEADS + 1, dtype=F32) / DIFF_HEADS)
            lam_params = jnp.stack([ab_lam_q1[j], ab_lam_k1[j], ab_lam_q2[j], ab_lam_k2[j]])
            qblk = g0 // LANES
            hd = _diff_attn(proj, slopes, lam_params, row(ab_dnorm_g[j]), tq=_tile(T, 512),
                            lam_init=lam_init, q0=qblk, k0=qblk + DIFF_HEADS, v0=qblk + 2 * DIFF_HEADS)
            wo = ab_w_out[j].astype(BF16)
            half = MLSTM_HEADS * MLSTM_DH
            xf = _proj_residual(xf, [hm.reshape(M, -1), hd.reshape(M, -1)], [wo[:half], wo[half:]],
                                tm=tm_proj)
        else:
            w = ssm_w_in[j]
            n_main = w.shape[1] - SSM_GROUPS * SSM_HPG
            w_main = w[:, :n_main].astype(BF16)
            w_dt = _pad_cols(w[:, n_main:], LANES).astype(BF16)
            inner = SSM_GROUPS * SSM_GW
            acts = (("silu", 1.0),) * (inner // tn) + (("conv", 1.0),) * ((n_main - inner) // tn)
            proj, dtraw = _norm_matmul(xf, row(norm_mix_g[i]), w_main, w_dt, ssm_conv_w[j],
                                       row(ssm_conv_b[j]), tm=tm_in, tn=tn, acts=acts, conv0=inner,
                                       tiles_per_seq=T // tm_in)
            head_params = jnp.pad(jnp.stack([ssm_dt_bias[j], ssm_a_log[j], ssm_d[j]]),
                                  ((0, HALO - 3), (0, LANES - SSM_GROUPS * SSM_HPG)))
            y = _ssd(proj.reshape(B, T, -1), dtraw.reshape(B, T, LANES), head_params,
                     row(ssm_norm_g[j]), L=_tile(T, 256), gps=8)
            xf = _proj_residual(xf, [y.reshape(M, -1)], [ssm_w_out[j].astype(BF16)], tm=tm_proj)
        xf = _mlp(xf, row(norm_mlp_g[i]), mlp_w1[i].astype(BF16), mlp_w2[i].astype(BF16),
                  row(final_g), tm=tm_mlp, tf=512, final_norm=(i == depth - 1))
    return xf.reshape(B, T, D)
```
